```python
import math
import jax, jax.numpy as jnp
from jax import lax
import numpy as np

D_MODEL = 1024
BATCH = 16
SEQ = 2048
DEPTH = 1

CHUNK = 64
QBLOCK = 128
PEER_BLOCK = 128

D_MIX = D_MODEL
SSD_WIDTH = D_MIX // 2
SSD_HEAD_DIM = 64
SSD_HEADS = SSD_WIDTH // SSD_HEAD_DIM
SSD_GROUPS = 2
SSD_HPG = SSD_HEADS // SSD_GROUPS
SSD_STATE = 128
SSD_CONV = 4
SSD_CONV_CH = SSD_WIDTH + 2 * SSD_GROUPS * SSD_STATE

DIFF_WIDTH = D_MIX - SSD_WIDTH
DIFF_HEAD_DIM = 64
DIFF_HEADS = DIFF_WIDTH // (2 * DIFF_HEAD_DIM)
ROPE_THETA = 10000.0

IN_WIDTH = SSD_WIDTH + SSD_CONV_CH + SSD_HEADS + 3 * DIFF_WIDTH

PEER_HEADS = 8
PEER_NKEYS = 128
PEER_EXPERTS = PEER_NKEYS * PEER_NKEYS
PEER_TOPK = 16
PEER_QDIM = 256
PEER_HALF = PEER_QDIM // 2

PLE_DIM = 256
EPS = 1e-6

kernel_name = "hybrid_ssd_diffattn_peer_block"


def rms_norm(x, g):
    xf = x.astype(jnp.float32)
    y = xf * lax.rsqrt(jnp.mean(xf * xf, axis=-1, keepdims=True) + EPS)
    return (y * g.astype(jnp.float32)).astype(x.dtype)


def causal_depthwise_conv(u, w, b):
    out = lax.conv_general_dilated(
        u, w[:, None, :].astype(u.dtype), window_strides=(1,),
        padding=[(w.shape[0] - 1, 0)],
        dimension_numbers=("NWC", "WIO", "NWC"),
        feature_group_count=u.shape[-1])
    return out + b.astype(u.dtype)


def ssd_mixer(z, xbc, dt_raw, conv_w, conv_b, dt_bias, a_log, d_skip, norm_g):
    b, L, _ = z.shape
    nc = L // CHUNK
    xbc = jax.nn.silu(causal_depthwise_conv(xbc, conv_w, conv_b))
    xs, bm, cm = jnp.split(xbc, [SSD_WIDTH, SSD_WIDTH + SSD_GROUPS * SSD_STATE], axis=-1)
    xs = xs.reshape(b, nc, CHUNK, SSD_GROUPS, SSD_HPG, SSD_HEAD_DIM)
    bm = bm.reshape(b, nc, CHUNK, SSD_GROUPS, SSD_STATE)
    cm = cm.reshape(b, nc, CHUNK, SSD_GROUPS, SSD_STATE)
    dt = jax.nn.softplus(dt_raw.astype(jnp.float32) + dt_bias.astype(jnp.float32))
    dt = dt.reshape(b, nc, CHUNK, SSD_GROUPS, SSD_HPG)
    a = -jnp.exp(a_log.astype(jnp.float32)).reshape(SSD_GROUPS, SSD_HPG)
    a_cs = jnp.cumsum(dt * a, axis=2)
    xdt = xs * dt[..., None]

    tri = jnp.tril(jnp.ones((CHUNK, CHUNK), dtype=bool))[:, :, None, None]
    seg = a_cs[:, :, :, None] - a_cs[:, :, None, :]
    lmat = jnp.exp(jnp.where(tri, seg, -jnp.inf))
    cb = jnp.einsum("bclgn,bcsgn->bclsg", cm, bm)
    y_diag = jnp.einsum("bclsg,bclsgr,bcsgrp->bclgrp", cb, lmat, xdt)

    decay_states = jnp.exp(a_cs[:, :, -1:] - a_cs)
    states = jnp.einsum("bclgn,bclgr,bclgrp->bcgrpn", bm, decay_states, xdt)
    chunk_decay = jnp.exp(a_cs[:, :, -1])

    def step(carry, inp):
        st, dec = inp
        return carry * dec[..., None, None] + st, carry

    init = jnp.zeros((b, SSD_GROUPS, SSD_HPG, SSD_HEAD_DIM, SSD_STATE), states.dtype)
    _, prev = lax.scan(step, init, (jnp.moveaxis(states, 1, 0), jnp.moveaxis(chunk_decay, 1, 0)))
    prev = jnp.moveaxis(prev, 0, 1)
    y_off = jnp.einsum("bclgn,bcgrpn,bclgr->bclgrp", cm, prev, jnp.exp(a_cs))

    y = y_diag + y_off + d_skip.reshape(SSD_GROUPS, SSD_HPG, 1) * xs
    y = y.reshape(b, L, SSD_WIDTH).astype(z.dtype)
    yg = (y * jax.nn.silu(z)).reshape(b, L, SSD_GROUPS, SSD_WIDTH // SSD_GROUPS)
    yg = rms_norm(yg, norm_g.reshape(SSD_GROUPS, -1))
    return yg.reshape(b, L, SSD_WIDTH)


def rope(x, cos, sin):
    half = x.shape[-1] // 2
    c = cos[:, :, None, None, :]
    s = sin[:, :, None, None, :]
    xf = x.astype(jnp.float32)
    x1, x2 = xf[..., :half], xf[..., half:]
    return jnp.concatenate([x1 * c - x2 * s, x2 * c + x1 * s], axis=-1).astype(x.dtype)


def diff_attention(q, k, v, positions, q_norm_g, k_norm_g, lam_vecs, out_norm_g, lam_init):
    b, L, _ = q.shape
    q = q.reshape(b, L, DIFF_HEADS, 2, DIFF_HEAD_DIM)
    k = k.reshape(b, L, DIFF_HEADS, 2, DIFF_HEAD_DIM)
    v = v.reshape(b, L, DIFF_HEADS, 2 * DIFF_HEAD_DIM)
    inv_freq = 1.0 / (ROPE_THETA ** (jnp.arange(0, DIFF_HEAD_DIM, 2, dtype=jnp.float32) / DIFF_HEAD_DIM))
    ang = positions[..., None].astype(jnp.float32) * inv_freq
    cos, sin = jnp.cos(ang), jnp.sin(ang)
    q = rope(rms_norm(q, q_norm_g), cos, sin)
    k = rope(rms_norm(k, k_norm_g), cos, sin)
    lv = lam_vecs.astype(jnp.float32)
    lam = jnp.exp(jnp.sum(lv[0] * lv[1])) - jnp.exp(jnp.sum(lv[2] * lv[3])) + lam_init
    scale = DIFF_HEAD_DIM ** -0.5
    outs = []
    for blk in range(L // QBLOCK):
        q0 = blk * QBLOCK
        kend = q0 + QBLOCK
        s = jnp.einsum("bqhcd,bkhcd->bhcqk", q[:, q0:kend], k[:, :kend]).astype(jnp.float32) * scale
        q_chunk = (q0 + jnp.arange(QBLOCK)) // CHUNK
        k_chunk = jnp.arange(kend) // CHUNK
        mask = k_chunk[None, :] <= q_chunk[:, None]
        probs = jax.nn.softmax(jnp.where(mask, s, -jnp.inf), axis=-1)
        attn = probs[:, :, 0] - lam * probs[:, :, 1]
        outs.append(jnp.einsum("bhqk,bkhe->bqhe", attn.astype(v.dtype), v[:, :kend]))
    o = jnp.concatenate(outs, axis=1)
    o = rms_norm(o, out_norm_g) * (1.0 - lam_init)
    return o.reshape(b, L, DIFF_WIDTH)


def peer_ffn(xn, wq, sub_k1, sub_k2, expert_u, expert_v):
    b, L, d = xn.shape
    xb = xn.reshape((b * L) // PEER_BLOCK, PEER_BLOCK, d)

    def block(xt):
        q = (xt @ wq).reshape(PEER_BLOCK, PEER_HEADS, 2, PEER_HALF)
        s1 = jnp.einsum("thd,kd->thk", q[:, :, 0], sub_k1).astype(jnp.float32)
        s2 = jnp.einsum("thd,kd->thk", q[:, :, 1], sub_k2).astype(jnp.float32)
        v1, i1 = lax.top_k(s1, PEER_TOPK)
        v2, i2 = lax.top_k(s2, PEER_TOPK)
        cand = (v1[..., :, None] + v2[..., None, :]).reshape(PEER_BLOCK, PEER_HEADS, PEER_TOPK * PEER_TOPK)
        cidx = (i1[..., :, None] * PEER_NKEYS + i2[..., None, :]).reshape(PEER_BLOCK, PEER_HEADS, PEER_TOPK * PEER_TOPK)
        top, pos = lax.top_k(cand, PEER_TOPK)
        eidx = jnp.take_along_axis(cidx, pos, axis=-1)
        g = jax.nn.softmax(top, axis=-1)
        u = expert_u[eidx]
        act = jax.nn.gelu(jnp.einsum("td,thkd->thk", xt, u), approximate=False)
        w = (act.astype(jnp.float32) * g).astype(xt.dtype)
        return jnp.einsum("thk,thkd->td", w, expert_v[eidx])

    return lax.map(block, xb).reshape(b, L, d)


def setup_inputs(seed: int = 0) -> dict:
    key = jax.random.key(seed)
    ks = jax.random.split(key, 32)

    def nrm(k, shape, scale):
        return jax.random.normal(k, shape, jnp.float32) * scale

    def gain(k, shape):
        return 1.0 + 0.05 * jax.random.normal(k, shape, jnp.float32)

    x = jax.random.normal(ks[0], (BATCH, SEQ, D_MODEL), jnp.float32)
    p = jax.random.normal(ks[1], (DEPTH, BATCH, SEQ, PLE_DIM), jnp.float32)
    offsets = jax.random.randint(ks[2], (BATCH, 1), 0, 64, dtype=jnp.int32) * CHUNK
    positions = (offsets + jnp.arange(SEQ, dtype=jnp.int32)[None, :]).astype(jnp.int32)

    dt = jnp.exp(jax.random.uniform(ks[3], (DEPTH, SSD_HEADS), jnp.float32)
                 * (math.log(0.1) - math.log(0.001)) + math.log(0.001))
    dt_bias = dt + jnp.log(-jnp.expm1(-dt))
    a_log = jnp.log(jax.random.uniform(ks[4], (DEPTH, SSD_HEADS), jnp.float32, 1.0, 16.0))

    return {
        "x": x,
        "p": p,
        "positions": positions,
        "norm_mix_g": gain(ks[5], (DEPTH, D_MODEL)),
        "w_in": nrm(ks[6], (DEPTH, D_MODEL, IN_WIDTH), D_MODEL ** -0.5),
        "conv_w": nrm(ks[7], (DEPTH, SSD_CONV, SSD_CONV_CH), SSD_CONV ** -0.5),
        "conv_b": nrm(ks[8], (DEPTH, SSD_CONV_CH), 0.02),
        "dt_bias": dt_bias,
        "a_log": a_log,
        "d_skip": gain(ks[9], (DEPTH, SSD_HEADS)),
        "ssd_norm_g": gain(ks[10], (DEPTH, SSD_WIDTH)),
        "q_norm_g": gain(ks[11], (DEPTH, DIFF_HEAD_DIM)),
        "k_norm_g": gain(ks[12], (DEPTH, DIFF_HEAD_DIM)),
        "lam_vecs": nrm(ks[13], (DEPTH, 4, DIFF_HEAD_DIM), 0.1),
        "diff_norm_g": gain(ks[14], (DEPTH, 2 * DIFF_HEAD_DIM)),
        "w_out": nrm(ks[15], (DEPTH, D_MIX, D_MODEL), D_MIX ** -0.5),
        "norm_ffn_g": gain(ks[16], (DEPTH, D_MODEL)),
        "peer_wq": nrm(ks[17], (DEPTH, D_MODEL, PEER_HEADS * PEER_QDIM), D_MODEL ** -0.5),
        "peer_k1": nrm(ks[18], (DEPTH, PEER_NKEYS, PEER_HALF), PEER_HALF ** -0.5),
        "peer_k2": nrm(ks[19], (DEPTH, PEER_NKEYS, PEER_HALF), PEER_HALF ** -0.5),
        "peer_u": nrm(ks[20], (DEPTH, PEER_EXPERTS, D_MODEL), D_MODEL ** -0.5),
        "peer_v": nrm(ks[21], (DEPTH, PEER_EXPERTS, D_MODEL), (PEER_HEADS * PEER_TOPK) ** -0.5),
        "ple_proj": nrm(ks[22], (DEPTH, PLE_DIM, D_MODEL), PLE_DIM ** -0.5),
        "ple_norm_g": gain(ks[23], (DEPTH, D_MODEL)),
        "ple_gate_norm_g": gain(ks[24], (DEPTH, D_MODEL)),
        "ple_gate_w": nrm(ks[25], (DEPTH, D_MODEL, D_MODEL), D_MODEL ** -0.5),
    }


def reference(x, p, positions, norm_mix_g, w_in, conv_w, conv_b, dt_bias, a_log, d_skip,
              ssd_norm_g, q_norm_g, k_norm_g, lam_vecs, diff_norm_g, w_out, norm_ffn_g,
              peer_wq, peer_k1, peer_k2, peer_u, peer_v, ple_proj, ple_norm_g,
              ple_gate_norm_g, ple_gate_w):
    h = x
    splits = np.cumsum([SSD_WIDTH, SSD_CONV_CH, SSD_HEADS, DIFF_WIDTH, DIFF_WIDTH]).tolist()
    for i in range(DEPTH):
        lam_init = 0.8 - 0.6 * math.exp(-0.3 * i)
        n = rms_norm(h, norm_mix_g[i])
        z, xbc, dt_raw, q, k, v = jnp.split(n @ w_in[i], splits, axis=-1)
        y_ssd = ssd_mixer(z, xbc, dt_raw, conv_w[i], conv_b[i], dt_bias[i], a_log[i],
                          d_skip[i], ssd_norm_g[i])
        y_diff = diff_attention(q, k, v, positions, q_norm_g[i], k_norm_g[i], lam_vecs[i],
                                diff_norm_g[i], lam_init)
        h = h + jnp.concatenate([y_ssd, y_diff], axis=-1) @ w_out[i]
        h = h + peer_ffn(rms_norm(h, norm_ffn_g[i]), peer_wq[i], peer_k1[i], peer_k2[i],
                         peer_u[i], peer_v[i])
        e = rms_norm(p[i] @ ple_proj[i], ple_norm_g[i])
        gate = jax.nn.sigmoid(rms_norm(h, ple_gate_norm_g[i]) @ ple_gate_w[i])
        h = h + gate * e
    return h
```

```python
import functools
import math

import numpy as np
import jax
import jax.numpy as jnp
from jax import lax
from jax.experimental import pallas as pl
from jax.experimental.pallas import tpu as pltpu

F32 = jnp.float32
BF16 = jnp.bfloat16
HIGHEST = lax.Precision.HIGHEST

D_MODEL = 1024
CHUNK = 64
QBLOCK = 128
SSD_WIDTH = 512
SSD_HEAD_DIM = 64
SSD_HEADS = 8
SSD_GROUPS = 2
SSD_HPG = 4
SSD_STATE = 128
SSD_CONV = 4
SSD_CONV_CH = 1024
SSD_GROUP_WIDTH = SSD_WIDTH // SSD_GROUPS
DIFF_WIDTH = 512
DIFF_HEAD_DIM = 64
DIFF_HEADS = 4
ROPE_THETA = 10000.0
PEER_HEADS = 8
PEER_NKEYS = 128
PEER_EXPERTS = PEER_NKEYS * PEER_NKEYS
PEER_TOPK = 16
PEER_QDIM = 256
PEER_HALF = 128
PEER_PAIRS = PEER_HEADS * PEER_TOPK
PLE_DIM = 256
EPS = 1e-6

LANES = 128
SUBLANES = 8
BF16_SUBLANES = 16
ROW_CHUNKS = D_MODEL // LANES
EXPERTS_PER_TILE = BF16_SUBLANES // ROW_CHUNKS
VMEM_LIMIT_TABLE = 56 * 1024 * 1024
VMEM_LIMIT_DENSE = 48 * 1024 * 1024
DT_PAD = LANES


def _cparams(sem, vmem=VMEM_LIMIT_DENSE):
    return pltpu.CompilerParams(dimension_semantics=sem, vmem_limit_bytes=vmem)


def _rms(x, g):
    ms = jnp.mean(x * x, axis=-1, keepdims=True)
    return x * lax.rsqrt(ms + EPS) * g


def _silu(x):
    return x * (1.0 / (1.0 + jnp.exp(-x)))


def _in_proj_kernel(x_ref, g_ref, wz_ref, wxbc_ref, wq_ref, wk_ref, wv_ref, wdt_ref,
                    z_ref, xbc_ref, q_ref, k_ref, v_ref, dt_ref):
    xn = _rms(x_ref[...], g_ref[...]).astype(BF16)
    for w_ref, o_ref in ((wz_ref, z_ref), (wxbc_ref, xbc_ref), (wq_ref, q_ref),
                         (wk_ref, k_ref), (wv_ref, v_ref), (wdt_ref, dt_ref)):
        o_ref[...] = jnp.dot(xn, w_ref[...], preferred_element_type=F32).astype(o_ref.dtype)


def _in_proj(x2, g, w_in, tb):
    t = x2.shape[0]
    o = np.cumsum([0, SSD_WIDTH, SSD_CONV_CH, SSD_HEADS, DIFF_WIDTH, DIFF_WIDTH, DIFF_WIDTH])
    wb = w_in.astype(BF16)
    wz, wxbc, wdt, wq, wk, wv = (wb[:, o[i]:o[i + 1]] for i in range(6))
    wdt = jnp.pad(wdt, ((0, 0), (0, DT_PAD - SSD_HEADS)))
    ws = (wz, wxbc, wq, wk, wv, wdt)
    widths = [w.shape[1] for w in ws]
    dtypes = [F32, F32, F32, F32, BF16, F32]
    full = lambda i: (0, 0)
    return pl.pallas_call(
        _in_proj_kernel,
        grid=(t // tb,),
        in_specs=[pl.BlockSpec((tb, D_MODEL), lambda i: (i, 0)),
                  pl.BlockSpec((1, D_MODEL), full)]
                 + [pl.BlockSpec((D_MODEL, n), full) for n in widths],
        out_specs=[pl.BlockSpec((tb, n), lambda i: (i, 0)) for n in widths],
        out_shape=[jax.ShapeDtypeStruct((t, n), d) for n, d in zip(widths, dtypes)],
        compiler_params=_cparams(("parallel",)),
        name="in_proj",
    )(x2, g.reshape(1, D_MODEL), *ws)


def _ssd_kernel(z_ref, xbc_ref, dt_ref, convw_ref, convb_ref, dtb_ref, alog_ref, dskip_ref,
                ng_ref, expand_ref, y_ref, ext_ref, state_ref):
    c = pl.program_id(1)

    @pl.when(c == 0)
    def _():
        ext_ref[0:SUBLANES, :] = jnp.zeros((SUBLANES, SSD_CONV_CH), F32)
        state_ref[...] = jnp.zeros_like(state_ref)

    ext_ref[SUBLANES:SUBLANES + CHUNK, :] = xbc_ref[...]
    conv = jnp.zeros((CHUNK, SSD_CONV_CH), F32) + convb_ref[...]
    for j in range(SSD_CONV):
        off = SUBLANES - (SSD_CONV - 1) + j
        conv = conv + ext_ref[off:off + CHUNK, :] * convw_ref[j:j + 1, :]
    ext_ref[0:SUBLANES, :] = ext_ref[CHUNK:CHUNK + SUBLANES, :]
    xbc = _silu(conv)
    xs = xbc[:, :SSD_WIDTH]
    bm = xbc[:, SSD_WIDTH:SSD_WIDTH + SSD_GROUPS * SSD_STATE]
    cm = xbc[:, SSD_WIDTH + SSD_GROUPS * SSD_STATE:]

    lane = lax.broadcasted_iota(jnp.int32, (1, DT_PAD), 1)
    head_lane = lane < SSD_HEADS
    v = dt_ref[...] + dtb_ref[...]
    dt = jnp.maximum(v, 0.0) + jnp.log(1.0 + jnp.exp(-jnp.abs(v)))
    a = jnp.where(head_lane, -jnp.exp(alog_ref[...]), 0.0)
    da = jnp.where(head_lane, dt * a, 0.0)
    row = lax.broadcasted_iota(jnp.int32, (CHUNK, CHUNK), 0)
    col = lax.broadcasted_iota(jnp.int32, (CHUNK, CHUNK), 1)
    tri = row >= col
    a_cs = jnp.dot(tri.astype(F32), da, precision=HIGHEST, preferred_element_type=F32)
    a_last = a_cs[CHUNK - 1:CHUNK, :]
    a_cs_t = a_cs.T
    stacked = jnp.concatenate([dt, jnp.exp(a_cs), jnp.exp(a_last - a_cs)], axis=0)
    expd = jnp.dot(stacked, expand_ref[...], precision=HIGHEST, preferred_element_type=F32)
    dt_e = expd[0:CHUNK]
    ea_e = expd[CHUNK:2 * CHUNK]
    decay_e = expd[2 * CHUNK:3 * CHUNK]
    xdt = xs * dt_e
    xdd = (xdt * decay_e).astype(BF16)
    xdt_b = xdt.astype(BF16)

    ys = []
    for g in range(SSD_GROUPS):
        gsl = slice(g * SSD_GROUP_WIDTH, (g + 1) * SSD_GROUP_WIDTH)
        nsl = slice(g * SSD_STATE, (g + 1) * SSD_STATE)
        bm_g = bm[:, nsl].astype(BF16)
        cm_g = cm[:, nsl].astype(BF16)
        cb = lax.dot_general(cm_g, bm_g, (((1,), (1,)), ((), ())), preferred_element_type=F32)
        st = state_ref[g]
        y_off = jnp.dot(cm_g, st.astype(BF16), preferred_element_type=F32) * ea_e[:, gsl]
        y_heads = []
        for r in range(SSD_HPG):
            h = g * SSD_HPG + r
            seg = a_cs[:, h:h + 1] - a_cs_t[h:h + 1, :]
            lmat = jnp.exp(jnp.where(tri, seg, -jnp.inf))
            hsl = slice(h * SSD_HEAD_DIM, (h + 1) * SSD_HEAD_DIM)
            y_heads.append(jnp.dot((cb * lmat).astype(BF16), xdt_b[:, hsl],
                                   preferred_element_type=F32))
        ys.append(jnp.concatenate(y_heads, axis=1) + y_off)
        upd = lax.dot_general(bm_g, xdd[:, gsl], (((0,), (0,)), ((), ())),
                              preferred_element_type=F32)
        state_ref[g] = st * ea_e[CHUNK - 1:CHUNK, gsl] + upd
    y = jnp.concatenate(ys, axis=1) + dskip_ref[...] * xs
    yz = y * _silu(z_ref[...])
    outs = []
    for g in range(SSD_GROUPS):
        gsl = slice(g * SSD_GROUP_WIDTH, (g + 1) * SSD_GROUP_WIDTH)
        outs.append(_rms(yz[:, gsl], ng_ref[:, gsl]))
    y_ref[...] = jnp.concatenate(outs, axis=1).astype(y_ref.dtype)


def _ssd(z, xbc, dt, conv_w, conv_b, dt_bias, a_log, d_skip, norm_g, batch, seq):
    t = z.shape[0]
    nc = seq // CHUNK
    pad8 = lambda p: jnp.pad(p.reshape(1, SSD_HEADS), ((0, 0), (0, DT_PAD - SSD_HEADS)))
    expand = np.zeros((DT_PAD, SSD_WIDTH), np.float32)
    for h in range(SSD_HEADS):
        expand[h, h * SSD_HEAD_DIM:(h + 1) * SSD_HEAD_DIM] = 1.0
    dskip_e = jnp.repeat(d_skip, SSD_HEAD_DIM).reshape(1, SSD_WIDTH)
    blk = lambda w: pl.BlockSpec((CHUNK, w), lambda b, c: (b * nc + c, 0))
    full = lambda shp: pl.BlockSpec(shp, lambda b, c: (0,) * len(shp))
    return pl.pallas_call(
        _ssd_kernel,
        grid=(batch, nc),
        in_specs=[blk(SSD_WIDTH), blk(SSD_CONV_CH), blk(DT_PAD),
                  full((SSD_CONV, SSD_CONV_CH)), full((1, SSD_CONV_CH)),
                  full((1, DT_PAD)), full((1, DT_PAD)), full((1, SSD_WIDTH)),
                  full((1, SSD_WIDTH)), full((DT_PAD, SSD_WIDTH))],
        out_specs=blk(SSD_WIDTH),
        out_shape=jax.ShapeDtypeStruct((t, SSD_WIDTH), BF16),
        scratch_shapes=[pltpu.VMEM((CHUNK + SUBLANES, SSD_CONV_CH), F32),
                        pltpu.VMEM((SSD_GROUPS, SSD_STATE, SSD_GROUP_WIDTH), F32)],
        compiler_params=_cparams(("parallel", "arbitrary")),
        name="ssd",
    )(z, xbc, dt, conv_w, conv_b.reshape(1, SSD_CONV_CH), pad8(dt_bias), pad8(a_log),
      dskip_e, norm_g.reshape(1, SSD_WIDTH), jnp.asarray(expand))


def _qk_prep_kernel(q_ref, k_ref, pos_ref, qg_ref, kg_ref, invf_ref, sign_ref, seg_ref,
                    qo_ref, ko_ref):
    ang = pos_ref[...] * invf_ref[...]
    cos = jnp.cos(ang)
    sin = jnp.sin(ang) * sign_ref[...]
    lane = lax.broadcasted_iota(jnp.int32, (1, DIFF_WIDTH), 1)
    first_half = (lane % DIFF_HEAD_DIM) < (DIFF_HEAD_DIM // 2)
    seg = seg_ref[...]

    def prep(x, g, scale):
        sq = x * x
        hi = sq.astype(BF16)
        lo = (sq - hi.astype(F32)).astype(BF16)
        ssum = (jnp.dot(hi, seg, preferred_element_type=F32)
                + jnp.dot(lo, seg, preferred_element_type=F32))
        xn = x * lax.rsqrt(ssum * (1.0 / DIFF_HEAD_DIM) + EPS) * g
        swapped = jnp.where(first_half,
                            pltpu.roll(xn, DIFF_WIDTH - DIFF_HEAD_DIM // 2, 1),
                            pltpu.roll(xn, DIFF_HEAD_DIM // 2, 1))
        return ((xn * cos + swapped * sin) * scale).astype(BF16)

    qo_ref[...] = prep(q_ref[...], qg_ref[...], DIFF_HEAD_DIM ** -0.5)
    ko_ref[...] = prep(k_ref[...], kg_ref[...], 1.0)


def _qk_prep(q, k, pos, q_norm_g, k_norm_g, tb):
    t = q.shape[0]
    nseg = DIFF_WIDTH // DIFF_HEAD_DIM
    inv_freq = 1.0 / (ROPE_THETA ** (np.arange(0, DIFF_HEAD_DIM, 2, dtype=np.float32)
                                     / DIFF_HEAD_DIM))
    invf = np.tile(inv_freq.astype(np.float32), 2 * nseg).reshape(1, DIFF_WIDTH)
    sign = np.tile(np.concatenate([-np.ones(DIFF_HEAD_DIM // 2, np.float32),
                                   np.ones(DIFF_HEAD_DIM // 2, np.float32)]), nseg)
    segm = np.kron(np.eye(nseg, dtype=np.float32),
                   np.ones((DIFF_HEAD_DIM, DIFF_HEAD_DIM), np.float32))
    full = lambda shp: pl.BlockSpec(shp, lambda i: (0, 0))
    blk = pl.BlockSpec((tb, DIFF_WIDTH), lambda i: (i, 0))
    return pl.pallas_call(
        _qk_prep_kernel,
        grid=(t // tb,),
        in_specs=[blk, blk, pl.BlockSpec((tb, 1), lambda i: (i, 0)),
                  full((1, DIFF_WIDTH)), full((1, DIFF_WIDTH)), full((1, DIFF_WIDTH)),
                  full((1, DIFF_WIDTH)), full((DIFF_WIDTH, DIFF_WIDTH))],
        out_specs=[blk, blk],
        out_shape=[jax.ShapeDtypeStruct((t, DIFF_WIDTH), BF16)] * 2,
        compiler_params=_cparams(("parallel",)),
        name="qk_prep",
    )(q, k, pos, jnp.tile(q_norm_g, nseg).reshape(1, DIFF_WIDTH),
      jnp.tile(k_norm_g, nseg).reshape(1, DIFF_WIDTH), jnp.asarray(invf),
      jnp.asarray(sign.reshape(1, DIFF_WIDTH)), jnp.asarray(segm, dtype=BF16))


def _attn_kernel(q_ref, k_ref, v_ref, lam_ref, og_ref, o_ref, m_ref, l_ref, acc_ref,
                 *, lam_init):
    i = pl.program_id(1)
    lv = lam_ref[...]
    lam = (jnp.exp(jnp.sum(lv[0:1] * lv[1:2], axis=-1, keepdims=True))
           - jnp.exp(jnp.sum(lv[2:3] * lv[3:4], axis=-1, keepdims=True)) + lam_init)
    lane = lax.broadcasted_iota(jnp.int32, (QBLOCK, 2 * DIFF_HEAD_DIM), 1)
    qrow = lax.broadcasted_iota(jnp.int32, (2 * QBLOCK, QBLOCK), 0) % QBLOCK
    kcol = lax.broadcasted_iota(jnp.int32, (2 * QBLOCK, QBLOCK), 1)
    diag_mask = (kcol // CHUNK) <= (qrow // CHUNK)

    for h in range(DIFF_HEADS):
        hsl = slice(h * 2 * DIFF_HEAD_DIM, (h + 1) * 2 * DIFF_HEAD_DIM)
        qh = q_ref[:, hsl]
        zero = jnp.zeros_like(qh)
        qs = jnp.concatenate([jnp.where(lane < DIFF_HEAD_DIM, qh, zero),
                              jnp.where(lane >= DIFF_HEAD_DIM, qh, zero)], axis=0)
        m_ref[...] = jnp.full(m_ref.shape, -jnp.inf, F32)
        l_ref[...] = jnp.zeros(l_ref.shape, F32)
        acc_ref[...] = jnp.zeros(acc_ref.shape, F32)

        def step(j, masked):
            start = pl.multiple_of(j * QBLOCK, QBLOCK)
            kb = k_ref[pl.ds(start, QBLOCK), hsl]
            vb = v_ref[pl.ds(start, QBLOCK), hsl]
            s = lax.dot_general(qs, kb, (((1,), (1,)), ((), ())), preferred_element_type=F32)
            if masked:
                s = jnp.where(diag_mask, s, -jnp.inf)
            m_old = m_ref[...]
            m_new = jnp.maximum(m_old, jnp.max(s, axis=-1, keepdims=True))
            alpha = jnp.exp(m_old - m_new)
            p = jnp.exp(s - m_new)
            l_ref[...] = alpha * l_ref[...] + jnp.sum(p, axis=-1, keepdims=True)
            acc_ref[...] = alpha * acc_ref[...] + jnp.dot(p.astype(BF16), vb,
                                                          preferred_element_type=F32)
            m_ref[...] = m_new

        def body(j, carry):
            step(j, False)
            return carry

        lax.fori_loop(0, i, body, 0)
        step(i, True)
        o_all = acc_ref[...] / l_ref[...]
        o = o_all[0:QBLOCK] - lam * o_all[QBLOCK:2 * QBLOCK]
        o_ref[:, hsl] = (_rms(o, og_ref[...]) * (1.0 - lam_init)).astype(o_ref.dtype)


def _attention(qr, kr, v, lam_vecs, out_norm_g, batch, seq, lam_init):
    t = qr.shape[0]
    nq = seq // QBLOCK
    kv_spec = pl.BlockSpec((seq, DIFF_WIDTH), lambda b, i: (b, 0))
    q_spec = pl.BlockSpec((QBLOCK, DIFF_WIDTH), lambda b, i: (b * nq + i, 0))
    return pl.pallas_call(
        functools.partial(_attn_kernel, lam_init=lam_init),
        grid=(batch, nq),
        in_specs=[q_spec, kv_spec, kv_spec,
                  pl.BlockSpec((4, DIFF_HEAD_DIM), lambda b, i: (0, 0)),
                  pl.BlockSpec((1, 2 * DIFF_HEAD_DIM), lambda b, i: (0, 0))],
        out_specs=q_spec,
        out_shape=jax.ShapeDtypeStruct((t, DIFF_WIDTH), BF16),
        scratch_shapes=[pltpu.VMEM((2 * QBLOCK, 1), F32), pltpu.VMEM((2 * QBLOCK, 1), F32),
                        pltpu.VMEM((2 * QBLOCK, 2 * DIFF_HEAD_DIM), F32)],
        compiler_params=_cparams(("parallel", "arbitrary")),
        name="diff_attention",
    )(qr, kr, v, lam_vecs, out_norm_g.reshape(1, 2 * DIFF_HEAD_DIM))


def _out_proj_kernel(x_ref, ys_ref, yd_ref, wo1_ref, wo2_ref, g_ref, wq_ref, k1_ref, k2_ref,
                     h_ref, xn_ref, s_ref):
    h = (x_ref[...]
         + jnp.dot(ys_ref[...], wo1_ref[...], preferred_element_type=F32)
         + jnp.dot(yd_ref[...], wo2_ref[...], preferred_element_type=F32))
    h_ref[...] = h
    xn = _rms(h, g_ref[...])
    xn_ref[...] = xn
    qp = jnp.dot(xn.astype(BF16), wq_ref[...], preferred_element_type=F32).astype(BF16)
    for hd in range(PEER_HEADS):
        for half, kref in ((0, k1_ref), (1, k2_ref)):
            off = hd * PEER_QDIM + half * PEER_HALF
            s_ref[hd, half] = lax.dot_general(kref[...], qp[:, off:off + PEER_HALF],
                                              (((1,), (1,)), ((), ())),
                                              preferred_element_type=F32)


def _out_proj(x2, y_ssd, y_diff, w_out, norm_g, wq, k1, k2, tb):
    t = x2.shape[0]
    wo = w_out.astype(BF16)
    full = lambda shp: pl.BlockSpec(shp, lambda i: (0,) * len(shp))
    row = lambda w: pl.BlockSpec((tb, w), lambda i: (i, 0))
    return pl.pallas_call(
        _out_proj_kernel,
        grid=(t // tb,),
        in_specs=[row(D_MODEL), row(SSD_WIDTH), row(DIFF_WIDTH),
                  full((SSD_WIDTH, D_MODEL)), full((DIFF_WIDTH, D_MODEL)), full((1, D_MODEL)),
                  full((D_MODEL, PEER_HEADS * PEER_QDIM)),
                  full((PEER_NKEYS, PEER_HALF)), full((PEER_NKEYS, PEER_HALF))],
        out_specs=[row(D_MODEL), row(D_MODEL),
                   pl.BlockSpec((PEER_HEADS, 2, PEER_NKEYS, tb), lambda i: (0, 0, 0, i))],
        out_shape=[jax.ShapeDtypeStruct((t, D_MODEL), F32),
                   jax.ShapeDtypeStruct((t, D_MODEL), F32),
                   jax.ShapeDtypeStruct((PEER_HEADS, 2, PEER_NKEYS, t), F32)],
        compiler_params=_cparams(("parallel",)),
        name="out_proj_peer_scores",
    )(x2, y_ssd, y_diff, wo[:SSD_WIDTH], wo[SSD_WIDTH:], norm_g.reshape(1, D_MODEL),
      wq.astype(BF16), k1.astype(BF16), k2.astype(BF16))


def _topk_rows(s, k):
    n = s.shape[0]
    rows = lax.broadcasted_iota(jnp.int32, s.shape, 0)
    vals, idxs = [], []
    for _ in range(k):
        m = jnp.max(s, axis=0, keepdims=True)
        idx = jnp.min(jnp.where(s == m, rows, n), axis=0, keepdims=True)
        vals.append(m)
        idxs.append(idx)
        s = jnp.where(rows == idx, -jnp.inf, s)
    return jnp.concatenate(vals, axis=0), jnp.concatenate(idxs, axis=0)


def _topk_kernel(s_ref, eidx_ref, gate_ref):
    v1, i1 = _topk_rows(s_ref[0, 0], PEER_TOPK)
    v2, i2 = _topk_rows(s_ref[0, 1], PEER_TOPK)
    tb = v1.shape[1]
    cand = jnp.concatenate([v1[a:a + 1, :] + v2 for a in range(PEER_TOPK)], axis=0)
    cidx = jnp.concatenate([i1[a:a + 1, :] * PEER_NKEYS + i2 for a in range(PEER_TOPK)], axis=0)
    n = PEER_TOPK * PEER_TOPK
    rows = lax.broadcasted_iota(jnp.int32, (n, tb), 0)
    tops, eids = [], []
    s = cand
    for _ in range(PEER_TOPK):
        m = jnp.max(s, axis=0, keepdims=True)
        pos = jnp.min(jnp.where(s == m, rows, n), axis=0, keepdims=True)
        hit = rows == pos
        tops.append(m)
        eids.append(jnp.max(jnp.where(hit, cidx, -1), axis=0, keepdims=True))
        s = jnp.where(hit, -jnp.inf, s)
    top = jnp.concatenate(tops, axis=0)
    e = jnp.exp(top - top[0:1, :])
    gate_ref[0] = e / jnp.sum(e, axis=0, keepdims=True)
    eidx_ref[0] = jnp.concatenate(eids, axis=0)


def _topk(s_t, tb):
    t = s_t.shape[-1]
    out_spec = pl.BlockSpec((1, PEER_TOPK, tb), lambda h, i: (h, 0, i))
    return pl.pallas_call(
        _topk_kernel,
        grid=(PEER_HEADS, t // tb),
        in_specs=[pl.BlockSpec((1, 2, PEER_NKEYS, tb), lambda h, i: (h, 0, 0, i))],
        out_specs=[out_spec, out_spec],
        out_shape=[jax.ShapeDtypeStruct((PEER_HEADS, PEER_TOPK, t), jnp.int32),
                   jax.ShapeDtypeStruct((PEER_HEADS, PEER_TOPK, t), F32)],
        compiler_params=_cparams(("parallel", "parallel")),
        name="peer_topk",
    )(s_t)


def _gather_rows(idx_ref, tab_ref, rows_ref, t, scale=None):
    def body(k, carry):
        e = idx_ref[t, k]
        start = pl.multiple_of((e // EXPERTS_PER_TILE) * BF16_SUBLANES, BF16_SUBLANES)
        tile = tab_ref[pl.ds(start, BF16_SUBLANES), :].astype(F32)
        odd = jnp.full((ROW_CHUNKS, LANES), e % EXPERTS_PER_TILE, jnp.int32) > 0
        row = jnp.where(odd, tile[ROW_CHUNKS:], tile[:ROW_CHUNKS])
        if scale is not None:
            row = row * scale
        rows_ref[pl.ds(pl.multiple_of(k * ROW_CHUNKS, ROW_CHUNKS), ROW_CHUNKS), :] = row
        return carry
    lax.fori_loop(0, PEER_PAIRS, body, 0, unroll=8)


def _peer_u_kernel(idx_ref, x_ref, gate_ref, sel_ref, tab_hbm, w_ref, tab_ref, rows_ref, sem,
                   *, tb):
    @pl.when(pl.program_id(0) == 0)
    def _():
        cp = pltpu.make_async_copy(tab_hbm, tab_ref, sem)
        cp.start()
        cp.wait()

    ones = jnp.ones((SUBLANES, LANES), F32)

    def token(t, carry):
        _gather_rows(idx_ref, tab_ref, rows_ref, t, scale=x_ref[t])
        part = jnp.dot(sel_ref[...], rows_ref[...].astype(BF16),
                       preferred_element_type=F32)
        act = lax.dot_general(ones, part, (((1,), (1,)), ((), ())), precision=HIGHEST,
                              preferred_element_type=F32)[0:1]
        gelu = 0.5 * act * (1.0 + lax.erf(act * (1.0 / math.sqrt(2.0))))
        w_ref[pl.ds(t, 1), :] = gelu * gate_ref[pl.ds(t, 1), :]
        return carry

    lax.fori_loop(0, tb, token, 0)


def _peer_v_kernel(idx_ref, w_ref, exp_ref, tab_hbm, o_ref, tab_ref, rows_ref, wrep_ref, sem,
                   *, tb):
    @pl.when(pl.program_id(0) == 0)
    def _():
        cp = pltpu.make_async_copy(tab_hbm, tab_ref, sem)
        cp.start()
        cp.wait()

    wrep_ref[...] = jnp.dot(w_ref[...].astype(BF16), exp_ref[...], preferred_element_type=F32)
    sub = lax.broadcasted_iota(jnp.int32, (ROW_CHUNKS, PEER_PAIRS * ROW_CHUNKS), 0)
    lane = lax.broadcasted_iota(jnp.int32, (ROW_CHUNKS, PEER_PAIRS * ROW_CHUNKS), 1)
    diag = (lane % ROW_CHUNKS) == sub

    def token(t, carry):
        _gather_rows(idx_ref, tab_ref, rows_ref, t)
        wsel = jnp.where(diag, wrep_ref[pl.ds(t, 1), :], 0.0).astype(BF16)
        o_ref[t] = jnp.dot(wsel, rows_ref[...].astype(BF16), preferred_element_type=F32)
        return carry

    lax.fori_loop(0, tb, token, 0)


def _table_layout(tab):
    return tab.astype(BF16).reshape(PEER_EXPERTS * ROW_CHUNKS, LANES)


def _peer_scratch():
    return [pltpu.VMEM((PEER_EXPERTS * ROW_CHUNKS, LANES), BF16),
            pltpu.VMEM((PEER_PAIRS * ROW_CHUNKS, LANES), F32)]


def _peer_u(eidx, xn3, gate, peer_u, tb):
    t = eidx.shape[0]
    sel = np.kron(np.eye(PEER_PAIRS, dtype=np.float32), np.ones((1, ROW_CHUNKS), np.float32))
    full = lambda shp: pl.BlockSpec(shp, lambda i: (0,) * len(shp))
    pairs = pl.BlockSpec((tb, PEER_PAIRS), lambda i: (i, 0))
    return pl.pallas_call(
        functools.partial(_peer_u_kernel, tb=tb),
        grid=(t // tb,),
        in_specs=[pl.BlockSpec((tb, PEER_PAIRS), lambda i: (i, 0), memory_space=pltpu.SMEM),
                  pl.BlockSpec((tb, ROW_CHUNKS, LANES), lambda i: (i, 0, 0)),
                  pairs, full((PEER_PAIRS, PEER_PAIRS * ROW_CHUNKS)),
                  pl.BlockSpec(memory_space=pl.ANY)],
        out_specs=pairs,
        out_shape=jax.ShapeDtypeStruct((t, PEER_PAIRS), F32),
        scratch_shapes=_peer_scratch() + [pltpu.SemaphoreType.DMA(())],
        compiler_params=_cparams(("arbitrary",), VMEM_LIMIT_TABLE),
        name="peer_u",
    )(eidx, xn3, gate, jnp.asarray(sel, dtype=BF16), _table_layout(peer_u))


def _peer_v(eidx, w, peer_v, tb):
    t = eidx.shape[0]
    expand = np.kron(np.eye(PEER_PAIRS, dtype=np.float32), np.ones((1, ROW_CHUNKS), np.float32))
    pairs = pl.BlockSpec((tb, PEER_PAIRS), lambda i: (i, 0))
    return pl.pallas_call(
        functools.partial(_peer_v_kernel, tb=tb),
        grid=(t // tb,),
        in_specs=[pl.BlockSpec((tb, PEER_PAIRS), lambda i: (i, 0), memory_space=pltpu.SMEM),
                  pairs,
                  pl.BlockSpec((PEER_PAIRS, PEER_PAIRS * ROW_CHUNKS), lambda i: (0, 0)),
                  pl.BlockSpec(memory_space=pl.ANY)],
        out_specs=pl.BlockSpec((tb, ROW_CHUNKS, LANES), lambda i: (i, 0, 0)),
        out_shape=jax.ShapeDtypeStruct((t, ROW_CHUNKS, LANES), F32),
        scratch_shapes=_peer_scratch() + [pltpu.VMEM((tb, PEER_PAIRS * ROW_CHUNKS), F32),
                                          pltpu.SemaphoreType.DMA(())],
        compiler_params=_cparams(("arbitrary",), VMEM_LIMIT_TABLE),
        name="peer_v",
    )(eidx, w, jnp.asarray(expand, dtype=BF16), _table_layout(peer_v))


def _final_kernel(h_ref, peer_ref, p_ref, wple_ref, gple_ref, ggate_ref, wgate_ref, o_ref):
    h = h_ref[...] + peer_ref[...]
    e = _rms(jnp.dot(p_ref[...].astype(BF16), wple_ref[...], preferred_element_type=F32),
             gple_ref[...])
    logits = jnp.dot(_rms(h, ggate_ref[...]).astype(BF16), wgate_ref[...],
                     preferred_element_type=F32)
    gate = 1.0 / (1.0 + jnp.exp(-logits))
    o_ref[...] = h + gate * e


def _final(h1, peer_out, p2, ple_proj, ple_norm_g, gate_norm_g, gate_w, tb):
    t = h1.shape[0]
    full = lambda shp: pl.BlockSpec(shp, lambda i: (0, 0))
    row = lambda w: pl.BlockSpec((tb, w), lambda i: (i, 0))
    return pl.pallas_call(
        _final_kernel,
        grid=(t // tb,),
        in_specs=[row(D_MODEL), row(D_MODEL), row(PLE_DIM), full((PLE_DIM, D_MODEL)),
                  full((1, D_MODEL)), full((1, D_MODEL)), full((D_MODEL, D_MODEL))],
        out_specs=row(D_MODEL),
        out_shape=jax.ShapeDtypeStruct((t, D_MODEL), F32),
        compiler_params=_cparams(("parallel",)),
        name="final_gate",
    )(h1, peer_out, p2, ple_proj.astype(BF16), ple_norm_g.reshape(1, D_MODEL),
      gate_norm_g.reshape(1, D_MODEL), gate_w.astype(BF16))


def _block_rows(t, want):
    tb = min(want, t)
    assert t % tb == 0
    return tb


def kernel(x, p, positions, norm_mix_g, w_in, conv_w, conv_b, dt_bias, a_log, d_skip, ssd_norm_g, q_norm_g, k_norm_g, lam_vecs, diff_norm_g, w_out, norm_ffn_g, peer_wq, peer_k1, peer_k2, peer_u, peer_v, ple_proj, ple_norm_g, ple_gate_norm_g, ple_gate_w):
    batch, seq, d = x.shape
    depth = w_in.shape[0]
    assert d == D_MODEL and seq % QBLOCK == 0
    t = batch * seq
    h = x.reshape(t, D_MODEL)
    pos = positions.reshape(t, 1).astype(F32)
    for i in range(depth):
        lam_init = 0.8 - 0.6 * math.exp(-0.3 * i)
        z, xbc, q, k, v, dt = _in_proj(h, norm_mix_g[i], w_in[i], _block_rows(t, 512))
        y_ssd = _ssd(z, xbc, dt, conv_w[i], conv_b[i], dt_bias[i], a_log[i], d_skip[i],
                     ssd_norm_g[i], batch, seq)
        qr, kr = _qk_prep(q, k, pos, q_norm_g[i], k_norm_g[i], _block_rows(t, 512))
        y_diff = _attention(qr, kr, v, lam_vecs[i], diff_norm_g[i], batch, seq, lam_init)
        h1, xn, s_t = _out_proj(h, y_ssd, y_diff, w_out[i], norm_ffn_g[i], peer_wq[i],
                                peer_k1[i], peer_k2[i], _block_rows(t, 256))
        eidx_t, gate_t = _topk(s_t, _block_rows(t, 256))
        eidx = eidx_t.reshape(PEER_PAIRS, t).T
        gate = gate_t.reshape(PEER_PAIRS, t).T
        tbp = _block_rows(t, 128)
        w = _peer_u(eidx, xn.reshape(t, ROW_CHUNKS, LANES), gate, peer_u[i], tbp)
        peer_out = _peer_v(eidx, w, peer_v[i], tbp).reshape(t, D_MODEL)
        h = _final(h1, peer_out, p[i].reshape(t, PLE_DIM), ple_proj[i], ple_norm_g[i],
                   ple_gate_norm_g[i], ple_gate_w[i], _block_rows(t, 256))
    return h.reshape(batch, seq, D_MODEL)
```

```python
import functools
import math

import numpy as np
import jax
import jax.numpy as jnp
from jax import lax
from jax.experimental import pallas as pl
from jax.experimental.pallas import tpu as pltpu

F32 = jnp.float32
BF16 = jnp.bfloat16
HIGHEST = lax.Precision.HIGHEST

D_MODEL = 1024
CHUNK = 64
QBLOCK = 128
SSD_WIDTH = 512
SSD_HEAD_DIM = 64
SSD_HEADS = 8
SSD_GROUPS = 2
SSD_HPG = 4
SSD_STATE = 128
SSD_CONV = 4
SSD_CONV_CH = 1024
SSD_GROUP_WIDTH = SSD_WIDTH // SSD_GROUPS
DIFF_WIDTH = 512
DIFF_HEAD_DIM = 64
DIFF_HEADS = 4
ROPE_THETA = 10000.0
PEER_HEADS = 8
PEER_NKEYS = 128
PEER_EXPERTS = PEER_NKEYS * PEER_NKEYS
PEER_TOPK = 16
PEER_QDIM = 256
PEER_HALF = 128
PEER_PAIRS = PEER_HEADS * PEER_TOPK
PLE_DIM = 256
EPS = 1e-6

LANES = 128
SUBLANES = 8
BF16_SUBLANES = 16
ROW_CHUNKS = D_MODEL // LANES
ROW_WORDS = D_MODEL // (2 * LANES)
TABLE_ROWS = PEER_EXPERTS * ROW_WORDS + 2 * ROW_WORDS
VMEM_LIMIT_TABLE = 56 * 1024 * 1024
VMEM_LIMIT_DENSE = 48 * 1024 * 1024
DT_PAD = LANES


def _cparams(sem, vmem=VMEM_LIMIT_DENSE):
    return pltpu.CompilerParams(dimension_semantics=sem, vmem_limit_bytes=vmem)


def _rms(x, g):
    ms = jnp.mean(x * x, axis=-1, keepdims=True)
    return x * lax.rsqrt(ms + EPS) * g


def _silu(x):
    return x * (1.0 / (1.0 + jnp.exp(-x)))


def _in_proj_kernel(x_ref, g_ref, wz_ref, wxbc_ref, wq_ref, wk_ref, wv_ref, wdt_ref,
                    z_ref, xbc_ref, q_ref, k_ref, v_ref, dt_ref):
    xn = _rms(x_ref[...], g_ref[...]).astype(BF16)
    for w_ref, o_ref in ((wz_ref, z_ref), (wxbc_ref, xbc_ref), (wq_ref, q_ref),
                         (wk_ref, k_ref), (wv_ref, v_ref), (wdt_ref, dt_ref)):
        o_ref[...] = jnp.dot(xn, w_ref[...], preferred_element_type=F32).astype(o_ref.dtype)


def _in_proj(x2, g, w_in, tb):
    t = x2.shape[0]
    o = np.cumsum([0, SSD_WIDTH, SSD_CONV_CH, SSD_HEADS, DIFF_WIDTH, DIFF_WIDTH, DIFF_WIDTH])
    wb = w_in.astype(BF16)
    wz, wxbc, wdt, wq, wk, wv = (wb[:, o[i]:o[i + 1]] for i in range(6))
    wdt = jnp.pad(wdt, ((0, 0), (0, DT_PAD - SSD_HEADS)))
    ws = (wz, wxbc, wq, wk, wv, wdt)
    widths = [w.shape[1] for w in ws]
    dtypes = [F32, F32, F32, F32, BF16, F32]
    full = lambda i: (0, 0)
    return pl.pallas_call(
        _in_proj_kernel,
        grid=(t // tb,),
        in_specs=[pl.BlockSpec((tb, D_MODEL), lambda i: (i, 0)),
                  pl.BlockSpec((1, D_MODEL), full)]
                 + [pl.BlockSpec((D_MODEL, n), full) for n in widths],
        out_specs=[pl.BlockSpec((tb, n), lambda i: (i, 0)) for n in widths],
        out_shape=[jax.ShapeDtypeStruct((t, n), d) for n, d in zip(widths, dtypes)],
        compiler_params=_cparams(("parallel",)),
        name="in_proj",
    )(x2, g.reshape(1, D_MODEL), *ws)


def _ssd_kernel(z_ref, xbc_ref, dt_ref, convw_ref, convb_ref, dtb_ref, alog_ref, dskip_ref,
                ng_ref, expand_ref, y_ref, ext_ref, state_ref):
    c = pl.program_id(1)

    @pl.when(c == 0)
    def _():
        ext_ref[0:SUBLANES, :] = jnp.zeros((SUBLANES, SSD_CONV_CH), F32)
        state_ref[...] = jnp.zeros_like(state_ref)

    ext_ref[SUBLANES:SUBLANES + CHUNK, :] = xbc_ref[...]
    conv = jnp.zeros((CHUNK, SSD_CONV_CH), F32) + convb_ref[...]
    for j in range(SSD_CONV):
        off = SUBLANES - (SSD_CONV - 1) + j
        conv = conv + ext_ref[off:off + CHUNK, :] * convw_ref[j:j + 1, :]
    ext_ref[0:SUBLANES, :] = ext_ref[CHUNK:CHUNK + SUBLANES, :]
    xbc = _silu(conv)
    xs = xbc[:, :SSD_WIDTH]
    bm = xbc[:, SSD_WIDTH:SSD_WIDTH + SSD_GROUPS * SSD_STATE]
    cm = xbc[:, SSD_WIDTH + SSD_GROUPS * SSD_STATE:]

    lane = lax.broadcasted_iota(jnp.int32, (1, DT_PAD), 1)
    head_lane = lane < SSD_HEADS
    v = dt_ref[...] + dtb_ref[...]
    dt = jnp.maximum(v, 0.0) + jnp.log(1.0 + jnp.exp(-jnp.abs(v)))
    a = jnp.where(head_lane, -jnp.exp(alog_ref[...]), 0.0)
    da = jnp.where(head_lane, dt * a, 0.0)
    row = lax.broadcasted_iota(jnp.int32, (CHUNK, CHUNK), 0)
    col = lax.broadcasted_iota(jnp.int32, (CHUNK, CHUNK), 1)
    tri = row >= col
    a_cs = jnp.dot(tri.astype(F32), da, precision=HIGHEST, preferred_element_type=F32)
    a_last = a_cs[CHUNK - 1:CHUNK, :]
    a_cs_t = a_cs.T
    stacked = jnp.concatenate([dt, jnp.exp(a_cs), jnp.exp(a_last - a_cs)], axis=0)
    expd = jnp.dot(stacked, expand_ref[...], precision=HIGHEST, preferred_element_type=F32)
    dt_e = expd[0:CHUNK]
    ea_e = expd[CHUNK:2 * CHUNK]
    decay_e = expd[2 * CHUNK:3 * CHUNK]
    xdt = xs * dt_e
    xdd = (xdt * decay_e).astype(BF16)
    xdt_b = xdt.astype(BF16)

    ys = []
    for g in range(SSD_GROUPS):
        gsl = slice(g * SSD_GROUP_WIDTH, (g + 1) * SSD_GROUP_WIDTH)
        nsl = slice(g * SSD_STATE, (g + 1) * SSD_STATE)
        bm_g = bm[:, nsl].astype(BF16)
        cm_g = cm[:, nsl].astype(BF16)
        cb = lax.dot_general(cm_g, bm_g, (((1,), (1,)), ((), ())), preferred_element_type=F32)
        st = state_ref[g]
        y_off = jnp.dot(cm_g, st.astype(BF16), preferred_element_type=F32) * ea_e[:, gsl]
        y_heads = []
        for r in range(SSD_HPG):
            h = g * SSD_HPG + r
            seg = a_cs[:, h:h + 1] - a_cs_t[h:h + 1, :]
            lmat = jnp.exp(jnp.where(tri, seg, -jnp.inf))
            hsl = slice(h * SSD_HEAD_DIM, (h + 1) * SSD_HEAD_DIM)
            y_heads.append(jnp.dot((cb * lmat).astype(BF16), xdt_b[:, hsl],
                                   preferred_element_type=F32))
        ys.append(jnp.concatenate(y_heads, axis=1) + y_off)
        upd = lax.dot_general(bm_g, xdd[:, gsl], (((0,), (0,)), ((), ())),
                              preferred_element_type=F32)
        state_ref[g] = st * ea_e[CHUNK - 1:CHUNK, gsl] + upd
    y = jnp.concatenate(ys, axis=1) + dskip_ref[...] * xs
    yz = y * _silu(z_ref[...])
    outs = []
    for g in range(SSD_GROUPS):
        gsl = slice(g * SSD_GROUP_WIDTH, (g + 1) * SSD_GROUP_WIDTH)
        outs.append(_rms(yz[:, gsl], ng_ref[:, gsl]))
    y_ref[...] = jnp.concatenate(outs, axis=1).astype(y_ref.dtype)


def _ssd(z, xbc, dt, conv_w, conv_b, dt_bias, a_log, d_skip, norm_g, batch, seq):
    t = z.shape[0]
    nc = seq // CHUNK
    pad8 = lambda p: jnp.pad(p.reshape(1, SSD_HEADS), ((0, 0), (0, DT_PAD - SSD_HEADS)))
    expand = np.zeros((DT_PAD, SSD_WIDTH), np.float32)
    for h in range(SSD_HEADS):
        expand[h, h * SSD_HEAD_DIM:(h + 1) * SSD_HEAD_DIM] = 1.0
    dskip_e = jnp.repeat(d_skip, SSD_HEAD_DIM).reshape(1, SSD_WIDTH)
    blk = lambda w: pl.BlockSpec((CHUNK, w), lambda b, c: (b * nc + c, 0))
    full = lambda shp: pl.BlockSpec(shp, lambda b, c: (0,) * len(shp))
    return pl.pallas_call(
        _ssd_kernel,
        grid=(batch, nc),
        in_specs=[blk(SSD_WIDTH), blk(SSD_CONV_CH), blk(DT_PAD),
                  full((SSD_CONV, SSD_CONV_CH)), full((1, SSD_CONV_CH)),
                  full((1, DT_PAD)), full((1, DT_PAD)), full((1, SSD_WIDTH)),
                  full((1, SSD_WIDTH)), full((DT_PAD, SSD_WIDTH))],
        out_specs=blk(SSD_WIDTH),
        out_shape=jax.ShapeDtypeStruct((t, SSD_WIDTH), BF16),
        scratch_shapes=[pltpu.VMEM((CHUNK + SUBLANES, SSD_CONV_CH), F32),
                        pltpu.VMEM((SSD_GROUPS, SSD_STATE, SSD_GROUP_WIDTH), F32)],
        compiler_params=_cparams(("parallel", "arbitrary")),
        name="ssd",
    )(z, xbc, dt, conv_w, conv_b.reshape(1, SSD_CONV_CH), pad8(dt_bias), pad8(a_log),
      dskip_e, norm_g.reshape(1, SSD_WIDTH), jnp.asarray(expand))


def _qk_prep_kernel(q_ref, k_ref, pos_ref, qg_ref, kg_ref, invf_ref, sign_ref, seg_ref,
                    qo_ref, ko_ref):
    ang = pos_ref[...] * invf_ref[...]
    cos = jnp.cos(ang)
    sin = jnp.sin(ang) * sign_ref[...]
    lane = lax.broadcasted_iota(jnp.int32, (1, DIFF_WIDTH), 1)
    first_half = (lane % DIFF_HEAD_DIM) < (DIFF_HEAD_DIM // 2)
    seg = seg_ref[...]

    def prep(x, g, scale):
        sq = x * x
        hi = sq.astype(BF16)
        lo = (sq - hi.astype(F32)).astype(BF16)
        ssum = (jnp.dot(hi, seg, preferred_element_type=F32)
                + jnp.dot(lo, seg, preferred_element_type=F32))
        xn = x * lax.rsqrt(ssum * (1.0 / DIFF_HEAD_DIM) + EPS) * g
        swapped = jnp.where(first_half,
                            pltpu.roll(xn, DIFF_WIDTH - DIFF_HEAD_DIM // 2, 1),
                            pltpu.roll(xn, DIFF_HEAD_DIM // 2, 1))
        return ((xn * cos + swapped * sin) * scale).astype(BF16)

    qo_ref[...] = prep(q_ref[...], qg_ref[...], DIFF_HEAD_DIM ** -0.5)
    ko_ref[...] = prep(k_ref[...], kg_ref[...], 1.0)


def _qk_prep(q, k, pos, q_norm_g, k_norm_g, tb):
    t = q.shape[0]
    nseg = DIFF_WIDTH // DIFF_HEAD_DIM
    inv_freq = 1.0 / (ROPE_THETA ** (np.arange(0, DIFF_HEAD_DIM, 2, dtype=np.float32)
                                     / DIFF_HEAD_DIM))
    invf = np.tile(inv_freq.astype(np.float32), 2 * nseg).reshape(1, DIFF_WIDTH)
    sign = np.tile(np.concatenate([-np.ones(DIFF_HEAD_DIM // 2, np.float32),
                                   np.ones(DIFF_HEAD_DIM // 2, np.float32)]), nseg)
    segm = np.kron(np.eye(nseg, dtype=np.float32),
                   np.ones((DIFF_HEAD_DIM, DIFF_HEAD_DIM), np.float32))
    full = lambda shp: pl.BlockSpec(shp, lambda i: (0, 0))
    blk = pl.BlockSpec((tb, DIFF_WIDTH), lambda i: (i, 0))
    return pl.pallas_call(
        _qk_prep_kernel,
        grid=(t // tb,),
        in_specs=[blk, blk, pl.BlockSpec((tb, 1), lambda i: (i, 0)),
                  full((1, DIFF_WIDTH)), full((1, DIFF_WIDTH)), full((1, DIFF_WIDTH)),
                  full((1, DIFF_WIDTH)), full((DIFF_WIDTH, DIFF_WIDTH))],
        out_specs=[blk, blk],
        out_shape=[jax.ShapeDtypeStruct((t, DIFF_WIDTH), BF16)] * 2,
        compiler_params=_cparams(("parallel",)),
        name="qk_prep",
    )(q, k, pos, jnp.tile(q_norm_g, nseg).reshape(1, DIFF_WIDTH),
      jnp.tile(k_norm_g, nseg).reshape(1, DIFF_WIDTH), jnp.asarray(invf),
      jnp.asarray(sign.reshape(1, DIFF_WIDTH)), jnp.asarray(segm, dtype=BF16))


def _attn_kernel(q_ref, k_ref, v_ref, lam_ref, og_ref, o_ref, m_ref, l_ref, acc_ref,
                 *, lam_init):
    i = pl.program_id(1)
    lv = lam_ref[...]
    lam = (jnp.exp(jnp.sum(lv[0:1] * lv[1:2], axis=-1, keepdims=True))
           - jnp.exp(jnp.sum(lv[2:3] * lv[3:4], axis=-1, keepdims=True)) + lam_init)
    lane = lax.broadcasted_iota(jnp.int32, (QBLOCK, 2 * DIFF_HEAD_DIM), 1)
    qrow = lax.broadcasted_iota(jnp.int32, (2 * QBLOCK, QBLOCK), 0) % QBLOCK
    kcol = lax.broadcasted_iota(jnp.int32, (2 * QBLOCK, QBLOCK), 1)
    diag_mask = (kcol // CHUNK) <= (qrow // CHUNK)

    for h in range(DIFF_HEADS):
        hsl = slice(h * 2 * DIFF_HEAD_DIM, (h + 1) * 2 * DIFF_HEAD_DIM)
        qh = q_ref[:, hsl]
        zero = jnp.zeros_like(qh)
        qs = jnp.concatenate([jnp.where(lane < DIFF_HEAD_DIM, qh, zero),
                              jnp.where(lane >= DIFF_HEAD_DIM, qh, zero)], axis=0)
        m_ref[...] = jnp.full(m_ref.shape, -jnp.inf, F32)
        l_ref[...] = jnp.zeros(l_ref.shape, F32)
        acc_ref[...] = jnp.zeros(acc_ref.shape, F32)

        def step(j, masked):
            start = pl.multiple_of(j * QBLOCK, QBLOCK)
            kb = k_ref[pl.ds(start, QBLOCK), hsl]
            vb = v_ref[pl.ds(start, QBLOCK), hsl]
            s = lax.dot_general(qs, kb, (((1,), (1,)), ((), ())), preferred_element_type=F32)
            if masked:
                s = jnp.where(diag_mask, s, -jnp.inf)
            m_old = m_ref[...]
            m_new = jnp.maximum(m_old, jnp.max(s, axis=-1, keepdims=True))
            alpha = jnp.exp(m_old - m_new)
            p = jnp.exp(s - m_new)
            l_ref[...] = alpha * l_ref[...] + jnp.sum(p, axis=-1, keepdims=True)
            acc_ref[...] = alpha * acc_ref[...] + jnp.dot(p.astype(BF16), vb,
                                                          preferred_element_type=F32)
            m_ref[...] = m_new

        def body(j, carry):
            step(j, False)
            return carry

        lax.fori_loop(0, i, body, 0)
        step(i, True)
        o_all = acc_ref[...] / l_ref[...]
        o = o_all[0:QBLOCK] - lam * o_all[QBLOCK:2 * QBLOCK]
        o_ref[:, hsl] = (_rms(o, og_ref[...]) * (1.0 - lam_init)).astype(o_ref.dtype)


def _attention(qr, kr, v, lam_vecs, out_norm_g, batch, seq, lam_init):
    t = qr.shape[0]
    nq = seq // QBLOCK
    kv_spec = pl.BlockSpec((seq, DIFF_WIDTH), lambda b, i: (b, 0))
    q_spec = pl.BlockSpec((QBLOCK, DIFF_WIDTH), lambda b, i: (b * nq + i, 0))
    return pl.pallas_call(
        functools.partial(_attn_kernel, lam_init=lam_init),
        grid=(batch, nq),
        in_specs=[q_spec, kv_spec, kv_spec,
                  pl.BlockSpec((4, DIFF_HEAD_DIM), lambda b, i: (0, 0)),
                  pl.BlockSpec((1, 2 * DIFF_HEAD_DIM), lambda b, i: (0, 0))],
        out_specs=q_spec,
        out_shape=jax.ShapeDtypeStruct((t, DIFF_WIDTH), BF16),
        scratch_shapes=[pltpu.VMEM((2 * QBLOCK, 1), F32), pltpu.VMEM((2 * QBLOCK, 1), F32),
                        pltpu.VMEM((2 * QBLOCK, 2 * DIFF_HEAD_DIM), F32)],
        compiler_params=_cparams(("parallel", "arbitrary")),
        name="diff_attention",
    )(qr, kr, v, lam_vecs, out_norm_g.reshape(1, 2 * DIFF_HEAD_DIM))


def _out_proj_kernel(x_ref, ys_ref, yd_ref, wo1_ref, wo2_ref, g_ref, wq_ref, k1_ref, k2_ref,
                     h_ref, xn_ref, s_ref):
    h = (x_ref[...]
         + jnp.dot(ys_ref[...], wo1_ref[...], preferred_element_type=F32)
         + jnp.dot(yd_ref[...], wo2_ref[...], preferred_element_type=F32))
    h_ref[...] = h
    xn = _rms(h, g_ref[...])
    xn_ref[...] = xn
    qp = jnp.dot(xn.astype(BF16), wq_ref[...], preferred_element_type=F32).astype(BF16)
    for hd in range(PEER_HEADS):
        for half, kref in ((0, k1_ref), (1, k2_ref)):
            off = hd * PEER_QDIM + half * PEER_HALF
            s_ref[hd, half] = lax.dot_general(kref[...], qp[:, off:off + PEER_HALF],
                                              (((1,), (1,)), ((), ())),
                                              preferred_element_type=F32)


def _out_proj(x2, y_ssd, y_diff, w_out, norm_g, wq, k1, k2, tb):
    t = x2.shape[0]
    wo = w_out.astype(BF16)
    full = lambda shp: pl.BlockSpec(shp, lambda i: (0,) * len(shp))
    row = lambda w: pl.BlockSpec((tb, w), lambda i: (i, 0))
    return pl.pallas_call(
        _out_proj_kernel,
        grid=(t // tb,),
        in_specs=[row(D_MODEL), row(SSD_WIDTH), row(DIFF_WIDTH),
                  full((SSD_WIDTH, D_MODEL)), full((DIFF_WIDTH, D_MODEL)), full((1, D_MODEL)),
                  full((D_MODEL, PEER_HEADS * PEER_QDIM)),
                  full((PEER_NKEYS, PEER_HALF)), full((PEER_NKEYS, PEER_HALF))],
        out_specs=[row(D_MODEL), row(D_MODEL),
                   pl.BlockSpec((PEER_HEADS, 2, PEER_NKEYS, tb), lambda i: (0, 0, 0, i))],
        out_shape=[jax.ShapeDtypeStruct((t, D_MODEL), F32),
                   jax.ShapeDtypeStruct((t, D_MODEL), F32),
                   jax.ShapeDtypeStruct((PEER_HEADS, 2, PEER_NKEYS, t), F32)],
        compiler_params=_cparams(("parallel",)),
        name="out_proj_peer_scores",
    )(x2, y_ssd, y_diff, wo[:SSD_WIDTH], wo[SSD_WIDTH:], norm_g.reshape(1, D_MODEL),
      wq.astype(BF16), k1.astype(BF16), k2.astype(BF16))


def _topk_rows(s, k):
    n = s.shape[0]
    rows = lax.broadcasted_iota(jnp.int32, s.shape, 0)
    vals, idxs = [], []
    for _ in range(k):
        m = jnp.max(s, axis=0, keepdims=True)
        idx = jnp.min(jnp.where(s == m, rows, n), axis=0, keepdims=True)
        vals.append(m)
        idxs.append(idx)
        s = jnp.where(rows == idx, -jnp.inf, s)
    return jnp.concatenate(vals, axis=0), jnp.concatenate(idxs, axis=0)


def _topk_kernel(s_ref, eidx_ref, gate_ref):
    v1, i1 = _topk_rows(s_ref[0, 0], PEER_TOPK)
    v2, i2 = _topk_rows(s_ref[0, 1], PEER_TOPK)
    tb = v1.shape[1]
    cand = jnp.concatenate([v1[a:a + 1, :] + v2 for a in range(PEER_TOPK)], axis=0)
    cidx = jnp.concatenate([i1[a:a + 1, :] * PEER_NKEYS + i2 for a in range(PEER_TOPK)], axis=0)
    n = PEER_TOPK * PEER_TOPK
    rows = lax.broadcasted_iota(jnp.int32, (n, tb), 0)
    tops, eids = [], []
    s = cand
    for _ in range(PEER_TOPK):
        m = jnp.max(s, axis=0, keepdims=True)
        pos = jnp.min(jnp.where(s == m, rows, n), axis=0, keepdims=True)
        hit = rows == pos
        tops.append(m)
        eids.append(jnp.max(jnp.where(hit, cidx, -1), axis=0, keepdims=True))
        s = jnp.where(hit, -jnp.inf, s)
    top = jnp.concatenate(tops, axis=0)
    e = jnp.exp(top - top[0:1, :])
    gate_ref[0] = e / jnp.sum(e, axis=0, keepdims=True)
    eidx_ref[0] = jnp.concatenate(eids, axis=0) * ROW_WORDS


def _topk(s_t, tb):
    t = s_t.shape[-1]
    out_spec = pl.BlockSpec((1, PEER_TOPK, tb), lambda h, i: (h, 0, i))
    return pl.pallas_call(
        _topk_kernel,
        grid=(PEER_HEADS, t // tb),
        in_specs=[pl.BlockSpec((1, 2, PEER_NKEYS, tb), lambda h, i: (h, 0, 0, i))],
        out_specs=[out_spec, out_spec],
        out_shape=[jax.ShapeDtypeStruct((PEER_HEADS, PEER_TOPK, t), jnp.int32),
                   jax.ShapeDtypeStruct((PEER_HEADS, PEER_TOPK, t), F32)],
        compiler_params=_cparams(("parallel", "parallel")),
        name="peer_topk",
    )(s_t)


def _table_layout(tab):
    tb = tab.astype(BF16).reshape(PEER_EXPERTS, ROW_WORDS, 2, LANES)
    words = lax.bitcast_convert_type(jnp.swapaxes(tb, -1, -2), jnp.uint32)
    return jnp.pad(words.reshape(PEER_EXPERTS * ROW_WORDS, LANES),
                   ((ROW_WORDS, ROW_WORDS), (0, 0)))


def _gather_rows(idx_ref, tab_ref, rows_ref, base):
    low = lax.broadcasted_iota(jnp.int32, (SUBLANES, LANES), 0) < ROW_WORDS
    for k2 in range(PEER_PAIRS // 2):
        ra = idx_ref[base + 2 * k2]
        rb = idx_ref[base + 2 * k2 + 1]
        a = tab_ref[pl.ds(ra + ROW_WORDS, SUBLANES), :]
        b = tab_ref[pl.ds(rb, SUBLANES), :]
        rows_ref[k2 * SUBLANES:(k2 + 1) * SUBLANES, :] = jnp.where(low, a, b)


def _chunk_diag():
    sub = lax.broadcasted_iota(jnp.int32, (ROW_CHUNKS, PEER_PAIRS * ROW_CHUNKS), 0)
    lane = lax.broadcasted_iota(jnp.int32, (ROW_CHUNKS, PEER_PAIRS * ROW_CHUNKS), 1)
    return (lane % ROW_CHUNKS) == sub


def _split_bf16(x):
    hi = x.astype(BF16)
    return hi, (x - hi.astype(F32)).astype(BF16)


def _peer_u_kernel(idx_ref, x_ref, gate_ref, seg_ref, tab_ref, w_ref, rows_ref, z_ref, *, tb):
    diag = _chunk_diag()

    def token(t, carry):
        _gather_rows(idx_ref, tab_ref, rows_ref, t * PEER_PAIRS)
        rows = pltpu.bitcast(rows_ref[...], BF16)
        z = lax.dot_general(x_ref[t].astype(BF16), rows, (((1,), (1,)), ((), ())),
                            preferred_element_type=F32)
        z_ref[pl.ds(t, 1), :] = jnp.sum(jnp.where(diag, z, 0.0), axis=0, keepdims=True)
        return carry

    lax.fori_loop(0, tb, token, 0)
    hi, lo = _split_bf16(z_ref[...])
    act = (jnp.dot(hi, seg_ref[...], preferred_element_type=F32)
           + jnp.dot(lo, seg_ref[...], preferred_element_type=F32))
    gelu = 0.5 * act * (1.0 + lax.erf(act * (1.0 / math.sqrt(2.0))))
    w_ref[...] = gelu * gate_ref[...]


def _peer_v_kernel(idx_ref, w_ref, exp_ref, tab_ref, o_ref, rows_ref, whi_ref, wlo_ref, *, tb):
    hi, lo = _split_bf16(w_ref[...])
    whi_ref[...] = jnp.dot(hi, exp_ref[...], preferred_element_type=F32)
    wlo_ref[...] = jnp.dot(lo, exp_ref[...], preferred_element_type=F32)
    diag = _chunk_diag()

    def token(t, carry):
        _gather_rows(idx_ref, tab_ref, rows_ref, t * PEER_PAIRS)
        rows = pltpu.bitcast(rows_ref[...], BF16)
        wsel = jnp.concatenate([jnp.where(diag, whi_ref[pl.ds(t, 1), :], 0.0),
                                jnp.where(diag, wlo_ref[pl.ds(t, 1), :], 0.0)],
                               axis=0).astype(BF16)
        out = jnp.dot(wsel, rows, preferred_element_type=F32)
        o_ref[t] = out[:ROW_CHUNKS] + out[ROW_CHUNKS:]
        return carry

    lax.fori_loop(0, tb, token, 0)


def _peer_specs(tb):
    idx = pl.BlockSpec((tb * PEER_PAIRS,), lambda i: (i,), memory_space=pltpu.SMEM)
    pairs = pl.BlockSpec((tb, PEER_PAIRS), lambda i: (i, 0))
    table = pl.BlockSpec((TABLE_ROWS, LANES), lambda i: (0, 0), pipeline_mode=pl.Buffered(1))
    rows = pl.BlockSpec((tb, ROW_CHUNKS, LANES), lambda i: (i, 0, 0))
    return idx, pairs, table, rows


def _peer_u(eidx4, xn3, gate, peer_u, tb):
    t = xn3.shape[0]
    seg = np.kron(np.eye(PEER_PAIRS, dtype=np.float32), np.ones((ROW_CHUNKS, 1), np.float32))
    idx, pairs, table, rows = _peer_specs(tb)
    return pl.pallas_call(
        functools.partial(_peer_u_kernel, tb=tb),
        grid=(t // tb,),
        in_specs=[idx, rows, pairs,
                  pl.BlockSpec((PEER_PAIRS * ROW_CHUNKS, PEER_PAIRS), lambda i: (0, 0)), table],
        out_specs=pairs,
        out_shape=jax.ShapeDtypeStruct((t, PEER_PAIRS), F32),
        scratch_shapes=[pltpu.VMEM((PEER_PAIRS * ROW_WORDS, LANES), jnp.uint32),
                        pltpu.VMEM((tb, PEER_PAIRS * ROW_CHUNKS), F32)],
        compiler_params=_cparams(("arbitrary",), VMEM_LIMIT_TABLE),
        name="peer_u",
    )(eidx4, xn3, gate, jnp.asarray(seg, dtype=BF16), _table_layout(peer_u))


def _peer_v(eidx4, w, peer_v, tb):
    t = w.shape[0]
    expand = np.kron(np.eye(PEER_PAIRS, dtype=np.float32), np.ones((1, ROW_CHUNKS), np.float32))
    idx, pairs, table, rows = _peer_specs(tb)
    return pl.pallas_call(
        functools.partial(_peer_v_kernel, tb=tb),
        grid=(t // tb,),
        in_specs=[idx, pairs,
                  pl.BlockSpec((PEER_PAIRS, PEER_PAIRS * ROW_CHUNKS), lambda i: (0, 0)), table],
        out_specs=rows,
        out_shape=jax.ShapeDtypeStruct((t, ROW_CHUNKS, LANES), F32),
        scratch_shapes=[pltpu.VMEM((PEER_PAIRS * ROW_WORDS, LANES), jnp.uint32),
                        pltpu.VMEM((tb, PEER_PAIRS * ROW_CHUNKS), F32),
                        pltpu.VMEM((tb, PEER_PAIRS * ROW_CHUNKS), F32)],
        compiler_params=_cparams(("arbitrary",), VMEM_LIMIT_TABLE),
        name="peer_v",
    )(eidx4, w, jnp.asarray(expand, dtype=BF16), _table_layout(peer_v))


def _final_kernel(h_ref, peer_ref, p_ref, wple_ref, gple_ref, ggate_ref, wgate_ref, o_ref):
    h = h_ref[...] + peer_ref[...]
    e = _rms(jnp.dot(p_ref[...].astype(BF16), wple_ref[...], preferred_element_type=F32),
             gple_ref[...])
    logits = jnp.dot(_rms(h, ggate_ref[...]).astype(BF16), wgate_ref[...],
                     preferred_element_type=F32)
    gate = 1.0 / (1.0 + jnp.exp(-logits))
    o_ref[...] = h + gate * e


def _final(h1, peer_out, p2, ple_proj, ple_norm_g, gate_norm_g, gate_w, tb):
    t = h1.shape[0]
    full = lambda shp: pl.BlockSpec(shp, lambda i: (0, 0))
    row = lambda w: pl.BlockSpec((tb, w), lambda i: (i, 0))
    return pl.pallas_call(
        _final_kernel,
        grid=(t // tb,),
        in_specs=[row(D_MODEL), row(D_MODEL), row(PLE_DIM), full((PLE_DIM, D_MODEL)),
                  full((1, D_MODEL)), full((1, D_MODEL)), full((D_MODEL, D_MODEL))],
        out_specs=row(D_MODEL),
        out_shape=jax.ShapeDtypeStruct((t, D_MODEL), F32),
        compiler_params=_cparams(("parallel",)),
        name="final_gate",
    )(h1, peer_out, p2, ple_proj.astype(BF16), ple_norm_g.reshape(1, D_MODEL),
      gate_norm_g.reshape(1, D_MODEL), gate_w.astype(BF16))


def _block_rows(t, want):
    tb = min(want, t)
    assert t % tb == 0
    return tb


def kernel(x, p, positions, norm_mix_g, w_in, conv_w, conv_b, dt_bias, a_log, d_skip, ssd_norm_g, q_norm_g, k_norm_g, lam_vecs, diff_norm_g, w_out, norm_ffn_g, peer_wq, peer_k1, peer_k2, peer_u, peer_v, ple_proj, ple_norm_g, ple_gate_norm_g, ple_gate_w):
    batch, seq, d = x.shape
    depth = w_in.shape[0]
    assert d == D_MODEL and seq % QBLOCK == 0
    t = batch * seq
    h = x.reshape(t, D_MODEL)
    pos = positions.reshape(t, 1).astype(F32)
    for i in range(depth):
        lam_init = 0.8 - 0.6 * math.exp(-0.3 * i)
        z, xbc, q, k, v, dt = _in_proj(h, norm_mix_g[i], w_in[i], _block_rows(t, 512))
        y_ssd = _ssd(z, xbc, dt, conv_w[i], conv_b[i], dt_bias[i], a_log[i], d_skip[i],
                     ssd_norm_g[i], batch, seq)
        qr, kr = _qk_prep(q, k, pos, q_norm_g[i], k_norm_g[i], _block_rows(t, 512))
        y_diff = _attention(qr, kr, v, lam_vecs[i], diff_norm_g[i], batch, seq, lam_init)
        h1, xn, s_t = _out_proj(h, y_ssd, y_diff, w_out[i], norm_ffn_g[i], peer_wq[i],
                                peer_k1[i], peer_k2[i], _block_rows(t, 256))
        eidx_t, gate_t = _topk(s_t, _block_rows(t, 256))
        eidx4 = eidx_t.reshape(PEER_PAIRS, t).T.reshape(t * PEER_PAIRS)
        gate = gate_t.reshape(PEER_PAIRS, t).T
        tbp = _block_rows(t, 128)
        w = _peer_u(eidx4, xn.reshape(t, ROW_CHUNKS, LANES), gate, peer_u[i], tbp)
        peer_out = _peer_v(eidx4, w, peer_v[i], tbp).reshape(t, D_MODEL)
        h = _final(h1, peer_out, p[i].reshape(t, PLE_DIM), ple_proj[i], ple_norm_g[i],
                   ple_gate_norm_g[i], ple_gate_w[i], _block_rows(t, 256))
    return h.reshape(batch, seq, D_MODEL)
```

```python
import functools
import math

import numpy as np
import jax
import jax.numpy as jnp
from jax import lax
from jax.experimental import pallas as pl
from jax.experimental.pallas import tpu as pltpu

F32 = jnp.float32
BF16 = jnp.bfloat16
HIGHEST = lax.Precision.HIGHEST

D_MODEL = 1024
CHUNK = 64
ATT_TILE = 256
SSD_WIDTH = 512
SSD_HEAD_DIM = 64
SSD_HEADS = 8
SSD_GROUPS = 2
SSD_HPG = 4
SSD_STATE = 128
SSD_CONV = 4
SSD_CONV_CH = 1024
SSD_GROUP_WIDTH = SSD_WIDTH // SSD_GROUPS
DIFF_WIDTH = 512
DIFF_HEAD_DIM = 64
DIFF_HEADS = 4
ROPE_THETA = 10000.0
PEER_HEADS = 8
PEER_NKEYS = 128
PEER_EXPERTS = PEER_NKEYS * PEER_NKEYS
PEER_TOPK = 16
PEER_QDIM = 256
PEER_HALF = 128
PEER_PAIRS = PEER_HEADS * PEER_TOPK
PLE_DIM = 256
EPS = 1e-6

LANES = 128
SUBLANES = 8
BF16_SUBLANES = 16
ROW_CHUNKS = D_MODEL // LANES
ROW_WORDS = D_MODEL // (2 * LANES)
TABLE_ROWS = PEER_EXPERTS * ROW_WORDS + 2 * ROW_WORDS
TOKENS_PER_TRIP = 4
ROW_BUFFERS = 2
TOPK_TILES_PER_TRIP = 2
VMEM_LIMIT_TABLE = 56 * 1024 * 1024
VMEM_LIMIT_DENSE = 48 * 1024 * 1024
DT_PAD = LANES


def _cparams(sem, vmem=VMEM_LIMIT_DENSE):
    return pltpu.CompilerParams(dimension_semantics=sem, vmem_limit_bytes=vmem)


def _rms(x, g):
    ms = jnp.mean(x * x, axis=-1, keepdims=True)
    return x * lax.rsqrt(ms + EPS) * g


def _silu(x):
    return x * (1.0 / (1.0 + jnp.exp(-x)))


def _in_proj_kernel(x_ref, g_ref, wz_ref, wxbc_ref, wq_ref, wk_ref, wv_ref, wdt_ref,
                    z_ref, xbc_ref, q_ref, k_ref, v_ref, dt_ref):
    xn = _rms(x_ref[...], g_ref[...]).astype(BF16)
    for w_ref, o_ref in ((wz_ref, z_ref), (wxbc_ref, xbc_ref), (wq_ref, q_ref),
                         (wk_ref, k_ref), (wv_ref, v_ref), (wdt_ref, dt_ref)):
        o_ref[...] = jnp.dot(xn, w_ref[...], preferred_element_type=F32).astype(o_ref.dtype)


def _in_proj(x2, g, w_in, tb):
    t = x2.shape[0]
    o = np.cumsum([0, SSD_WIDTH, SSD_CONV_CH, SSD_HEADS, DIFF_WIDTH, DIFF_WIDTH, DIFF_WIDTH])
    wb = w_in.astype(BF16)
    wz, wxbc, wdt, wq, wk, wv = (wb[:, o[i]:o[i + 1]] for i in range(6))
    wdt = jnp.pad(wdt, ((0, 0), (0, DT_PAD - SSD_HEADS)))
    ws = (wz, wxbc, wq, wk, wv, wdt)
    widths = [w.shape[1] for w in ws]
    dtypes = [F32, F32, F32, F32, BF16, F32]
    full = lambda i: (0, 0)
    return pl.pallas_call(
        _in_proj_kernel,
        grid=(t // tb,),
        in_specs=[pl.BlockSpec((tb, D_MODEL), lambda i: (i, 0)),
                  pl.BlockSpec((1, D_MODEL), full)]
                 + [pl.BlockSpec((D_MODEL, n), full) for n in widths],
        out_specs=[pl.BlockSpec((tb, n), lambda i: (i, 0)) for n in widths],
        out_shape=[jax.ShapeDtypeStruct((t, n), d) for n, d in zip(widths, dtypes)],
        compiler_params=_cparams(("parallel",)),
        name="in_proj",
    )(x2, g.reshape(1, D_MODEL), *ws)


def _ssd_kernel(z_ref, xbc_ref, dt_ref, convw_ref, convb_ref, dtb_ref, alog_ref, dskip_ref,
                ng_ref, expand_ref, y_ref, ext_ref, state_ref):
    c = pl.program_id(1)

    @pl.when(c == 0)
    def _():
        ext_ref[0:SUBLANES, :] = jnp.zeros((SUBLANES, SSD_CONV_CH), F32)
        state_ref[...] = jnp.zeros_like(state_ref)

    ext_ref[SUBLANES:SUBLANES + CHUNK, :] = xbc_ref[...]
    conv = jnp.zeros((CHUNK, SSD_CONV_CH), F32) + convb_ref[...]
    for j in range(SSD_CONV):
        off = SUBLANES - (SSD_CONV - 1) + j
        conv = conv + ext_ref[off:off + CHUNK, :] * convw_ref[j:j + 1, :]
    ext_ref[0:SUBLANES, :] = ext_ref[CHUNK:CHUNK + SUBLANES, :]
    xbc = _silu(conv)
    xs = xbc[:, :SSD_WIDTH]
    bm = xbc[:, SSD_WIDTH:SSD_WIDTH + SSD_GROUPS * SSD_STATE]
    cm = xbc[:, SSD_WIDTH + SSD_GROUPS * SSD_STATE:]

    lane = lax.broadcasted_iota(jnp.int32, (1, DT_PAD), 1)
    head_lane = lane < SSD_HEADS
    v = dt_ref[...] + dtb_ref[...]
    dt = jnp.maximum(v, 0.0) + jnp.log(1.0 + jnp.exp(-jnp.abs(v)))
    a = jnp.where(head_lane, -jnp.exp(alog_ref[...]), 0.0)
    da = jnp.where(head_lane, dt * a, 0.0)
    row = lax.broadcasted_iota(jnp.int32, (CHUNK, CHUNK), 0)
    col = lax.broadcasted_iota(jnp.int32, (CHUNK, CHUNK), 1)
    tri = row >= col
    a_cs = jnp.dot(tri.astype(F32), da, precision=HIGHEST, preferred_element_type=F32)
    a_last = a_cs[CHUNK - 1:CHUNK, :]
    a_cs_t = a_cs.T
    stacked = jnp.concatenate([dt, jnp.exp(a_cs), jnp.exp(a_last - a_cs)], axis=0)
    expd = jnp.dot(stacked, expand_ref[...], precision=HIGHEST, preferred_element_type=F32)
    dt_e = expd[0:CHUNK]
    ea_e = expd[CHUNK:2 * CHUNK]
    decay_e = expd[2 * CHUNK:3 * CHUNK]
    xdt = xs * dt_e
    xdd = (xdt * decay_e).astype(BF16)
    xdt_b = xdt.astype(BF16)

    ys = []
    for g in range(SSD_GROUPS):
        gsl = slice(g * SSD_GROUP_WIDTH, (g + 1) * SSD_GROUP_WIDTH)
        nsl = slice(g * SSD_STATE, (g + 1) * SSD_STATE)
        bm_g = bm[:, nsl].astype(BF16)
        cm_g = cm[:, nsl].astype(BF16)
        cb = lax.dot_general(cm_g, bm_g, (((1,), (1,)), ((), ())), preferred_element_type=F32)
        st = state_ref[g]
        y_off = jnp.dot(cm_g, st.astype(BF16), preferred_element_type=F32) * ea_e[:, gsl]
        y_heads = []
        for r in range(SSD_HPG):
            h = g * SSD_HPG + r
            seg = a_cs[:, h:h + 1] - a_cs_t[h:h + 1, :]
            lmat = jnp.exp(jnp.where(tri, seg, -jnp.inf))
            hsl = slice(h * SSD_HEAD_DIM, (h + 1) * SSD_HEAD_DIM)
            y_heads.append(jnp.dot((cb * lmat).astype(BF16), xdt_b[:, hsl],
                                   preferred_element_type=F32))
        ys.append(jnp.concatenate(y_heads, axis=1) + y_off)
        upd = lax.dot_general(bm_g, xdd[:, gsl], (((0,), (0,)), ((), ())),
                              preferred_element_type=F32)
        state_ref[g] = st * ea_e[CHUNK - 1:CHUNK, gsl] + upd
    y = jnp.concatenate(ys, axis=1) + dskip_ref[...] * xs
    yz = y * _silu(z_ref[...])
    outs = []
    for g in range(SSD_GROUPS):
        gsl = slice(g * SSD_GROUP_WIDTH, (g + 1) * SSD_GROUP_WIDTH)
        outs.append(_rms(yz[:, gsl], ng_ref[:, gsl]))
    y_ref[...] = jnp.concatenate(outs, axis=1).astype(y_ref.dtype)


def _ssd(z, xbc, dt, conv_w, conv_b, dt_bias, a_log, d_skip, norm_g, batch, seq):
    t = z.shape[0]
    nc = seq // CHUNK
    pad8 = lambda p: jnp.pad(p.reshape(1, SSD_HEADS), ((0, 0), (0, DT_PAD - SSD_HEADS)))
    expand = np.zeros((DT_PAD, SSD_WIDTH), np.float32)
    for h in range(SSD_HEADS):
        expand[h, h * SSD_HEAD_DIM:(h + 1) * SSD_HEAD_DIM] = 1.0
    dskip_e = jnp.repeat(d_skip, SSD_HEAD_DIM).reshape(1, SSD_WIDTH)
    blk = lambda w: pl.BlockSpec((CHUNK, w), lambda b, c: (b * nc + c, 0))
    full = lambda shp: pl.BlockSpec(shp, lambda b, c: (0,) * len(shp))
    return pl.pallas_call(
        _ssd_kernel,
        grid=(batch, nc),
        in_specs=[blk(SSD_WIDTH), blk(SSD_CONV_CH), blk(DT_PAD),
                  full((SSD_CONV, SSD_CONV_CH)), full((1, SSD_CONV_CH)),
                  full((1, DT_PAD)), full((1, DT_PAD)), full((1, SSD_WIDTH)),
                  full((1, SSD_WIDTH)), full((DT_PAD, SSD_WIDTH))],
        out_specs=blk(SSD_WIDTH),
        out_shape=jax.ShapeDtypeStruct((t, SSD_WIDTH), BF16),
        scratch_shapes=[pltpu.VMEM((CHUNK + SUBLANES, SSD_CONV_CH), F32),
                        pltpu.VMEM((SSD_GROUPS, SSD_STATE, SSD_GROUP_WIDTH), F32)],
        compiler_params=_cparams(("parallel", "arbitrary")),
        name="ssd",
    )(z, xbc, dt, conv_w, conv_b.reshape(1, SSD_CONV_CH), pad8(dt_bias), pad8(a_log),
      dskip_e, norm_g.reshape(1, SSD_WIDTH), jnp.asarray(expand))


def _qk_prep_kernel(q_ref, k_ref, pos_ref, qg_ref, kg_ref, invf_ref, sign_ref, seg_ref,
                    qo_ref, ko_ref):
    ang = pos_ref[...] * invf_ref[...]
    cos = jnp.cos(ang)
    sin = jnp.sin(ang) * sign_ref[...]
    lane = lax.broadcasted_iota(jnp.int32, (1, DIFF_WIDTH), 1)
    first_half = (lane % DIFF_HEAD_DIM) < (DIFF_HEAD_DIM // 2)
    seg = seg_ref[...]

    def prep(x, g, scale):
        sq = x * x
        hi = sq.astype(BF16)
        lo = (sq - hi.astype(F32)).astype(BF16)
        ssum = (jnp.dot(hi, seg, preferred_element_type=F32)
                + jnp.dot(lo, seg, preferred_element_type=F32))
        xn = x * lax.rsqrt(ssum * (1.0 / DIFF_HEAD_DIM) + EPS) * g
        swapped = jnp.where(first_half,
                            pltpu.roll(xn, DIFF_WIDTH - DIFF_HEAD_DIM // 2, 1),
                            pltpu.roll(xn, DIFF_HEAD_DIM // 2, 1))
        return ((xn * cos + swapped * sin) * scale).astype(BF16)

    qo_ref[...] = prep(q_ref[...], qg_ref[...], DIFF_HEAD_DIM ** -0.5)
    ko_ref[...] = prep(k_ref[...], kg_ref[...], 1.0)


def _qk_prep(q, k, pos, q_norm_g, k_norm_g, tb):
    t = q.shape[0]
    nseg = DIFF_WIDTH // DIFF_HEAD_DIM
    inv_freq = 1.0 / (ROPE_THETA ** (np.arange(0, DIFF_HEAD_DIM, 2, dtype=np.float32)
                                     / DIFF_HEAD_DIM))
    invf = np.tile(inv_freq.astype(np.float32), 2 * nseg).reshape(1, DIFF_WIDTH)
    sign = np.tile(np.concatenate([-np.ones(DIFF_HEAD_DIM // 2, np.float32),
                                   np.ones(DIFF_HEAD_DIM // 2, np.float32)]), nseg)
    segm = np.kron(np.eye(nseg, dtype=np.float32),
                   np.ones((DIFF_HEAD_DIM, DIFF_HEAD_DIM), np.float32))
    full = lambda shp: pl.BlockSpec(shp, lambda i: (0, 0))
    blk = pl.BlockSpec((tb, DIFF_WIDTH), lambda i: (i, 0))
    return pl.pallas_call(
        _qk_prep_kernel,
        grid=(t // tb,),
        in_specs=[blk, blk, pl.BlockSpec((tb, 1), lambda i: (i, 0)),
                  full((1, DIFF_WIDTH)), full((1, DIFF_WIDTH)), full((1, DIFF_WIDTH)),
                  full((1, DIFF_WIDTH)), full((DIFF_WIDTH, DIFF_WIDTH))],
        out_specs=[blk, blk],
        out_shape=[jax.ShapeDtypeStruct((t, DIFF_WIDTH), BF16)] * 2,
        compiler_params=_cparams(("parallel",)),
        name="qk_prep",
    )(q, k, pos, jnp.tile(q_norm_g, nseg).reshape(1, DIFF_WIDTH),
      jnp.tile(k_norm_g, nseg).reshape(1, DIFF_WIDTH), jnp.asarray(invf),
      jnp.asarray(sign.reshape(1, DIFF_WIDTH)), jnp.asarray(segm, dtype=BF16))


def _attn_kernel(q_ref, k_ref, v_ref, lam_ref, og_ref, o_ref, qs_ref, m_ref, acc_ref,
                 *, lam_init):
    i = pl.program_id(1)
    hd2 = 2 * DIFF_HEAD_DIM
    lv = lam_ref[...]
    lam = (jnp.exp(jnp.sum(lv[0:1] * lv[1:2], axis=-1, keepdims=True))
           - jnp.exp(jnp.sum(lv[2:3] * lv[3:4], axis=-1, keepdims=True)) + lam_init)
    lane = lax.broadcasted_iota(jnp.int32, (ATT_TILE, hd2), 1)
    ones = jnp.ones((ATT_TILE, hd2), BF16)
    heads = [slice(h * hd2, (h + 1) * hd2) for h in range(DIFF_HEADS)]

    for h, hsl in enumerate(heads):
        qh = q_ref[:, hsl]
        zero = jnp.zeros_like(qh)
        qs_ref[h] = jnp.concatenate([jnp.where(lane < DIFF_HEAD_DIM, qh, zero),
                                     jnp.where(lane >= DIFF_HEAD_DIM, qh, zero)], axis=0)
    m_ref[...] = jnp.full(m_ref.shape, -jnp.inf, F32)
    acc_ref[...] = jnp.zeros(acc_ref.shape, F32)

    def step(j, masked):
        start = pl.multiple_of(j * ATT_TILE, ATT_TILE)
        for h, hsl in enumerate(heads):
            kb = k_ref[pl.ds(start, ATT_TILE), hsl]
            vb = jnp.concatenate([v_ref[pl.ds(start, ATT_TILE), hsl], ones], axis=1)
            s = lax.dot_general(qs_ref[h], kb, (((1,), (1,)), ((), ())),
                                preferred_element_type=F32)
            if masked:
                qrow = lax.broadcasted_iota(jnp.int32, s.shape, 0) % ATT_TILE
                kcol = lax.broadcasted_iota(jnp.int32, s.shape, 1)
                s = jnp.where((kcol // CHUNK) <= (qrow // CHUNK), s, -jnp.inf)
            m_old = m_ref[h]
            m_new = jnp.maximum(m_old, jnp.max(s, axis=-1, keepdims=True))
            alpha = jnp.exp(m_old - m_new)
            p = jnp.exp(s - jnp.concatenate([m_new, m_new], axis=1))
            acc_ref[h] = (jnp.concatenate([alpha, alpha], axis=1) * acc_ref[h]
                          + jnp.dot(p.astype(BF16), vb, preferred_element_type=F32))
            m_ref[h] = m_new

    def body(j, carry):
        step(j, False)
        return carry

    lax.fori_loop(0, i, body, 0)
    step(i, True)
    for h, hsl in enumerate(heads):
        acc = acc_ref[h]
        o_all = acc[:, :hd2] / acc[:, hd2:]
        o = o_all[0:ATT_TILE] - lam * o_all[ATT_TILE:2 * ATT_TILE]
        o_ref[:, hsl] = (_rms(o, og_ref[...]) * (1.0 - lam_init)).astype(o_ref.dtype)


def _attention(qr, kr, v, lam_vecs, out_norm_g, batch, seq, lam_init):
    t = qr.shape[0]
    assert seq % ATT_TILE == 0
    nq = seq // ATT_TILE
    hd2 = 2 * DIFF_HEAD_DIM
    kv_spec = pl.BlockSpec((seq, DIFF_WIDTH), lambda b, i: (b, 0))
    q_spec = pl.BlockSpec((ATT_TILE, DIFF_WIDTH), lambda b, i: (b * nq + i, 0))
    return pl.pallas_call(
        functools.partial(_attn_kernel, lam_init=lam_init),
        grid=(batch, nq),
        in_specs=[q_spec, kv_spec, kv_spec,
                  pl.BlockSpec((4, DIFF_HEAD_DIM), lambda b, i: (0, 0)),
                  pl.BlockSpec((1, hd2), lambda b, i: (0, 0))],
        out_specs=q_spec,
        out_shape=jax.ShapeDtypeStruct((t, DIFF_WIDTH), BF16),
        scratch_shapes=[pltpu.VMEM((DIFF_HEADS, 2 * ATT_TILE, hd2), BF16),
                        pltpu.VMEM((DIFF_HEADS, 2 * ATT_TILE, hd2), F32),
                        pltpu.VMEM((DIFF_HEADS, 2 * ATT_TILE, 2 * hd2), F32)],
        compiler_params=_cparams(("parallel", "arbitrary")),
        name="diff_attention",
    )(qr, kr, v, lam_vecs, out_norm_g.reshape(1, hd2))


def _out_proj_kernel(x_ref, ys_ref, yd_ref, wo1_ref, wo2_ref, g_ref, wq_ref, k1_ref, k2_ref,
                     h_ref, xn_ref, s_ref):
    h = (x_ref[...]
         + jnp.dot(ys_ref[...], wo1_ref[...], preferred_element_type=F32)
         + jnp.dot(yd_ref[...], wo2_ref[...], preferred_element_type=F32))
    h_ref[...] = h
    xn = _rms(h, g_ref[...])
    xn_ref[...] = xn
    qp = jnp.dot(xn.astype(BF16), wq_ref[...], preferred_element_type=F32).astype(BF16)
    for hd in range(PEER_HEADS):
        for half, kref in ((0, k1_ref), (1, k2_ref)):
            off = hd * PEER_QDIM + half * PEER_HALF
            s_ref[hd, half] = lax.dot_general(kref[...], qp[:, off:off + PEER_HALF],
                                              (((1,), (1,)), ((), ())),
                                              preferred_element_type=F32)


def _out_proj(x2, y_ssd, y_diff, w_out, norm_g, wq, k1, k2, tb):
    t = x2.shape[0]
    wo = w_out.astype(BF16)
    full = lambda shp: pl.BlockSpec(shp, lambda i: (0,) * len(shp))
    row = lambda w: pl.BlockSpec((tb, w), lambda i: (i, 0))
    return pl.pallas_call(
        _out_proj_kernel,
        grid=(t // tb,),
        in_specs=[row(D_MODEL), row(SSD_WIDTH), row(DIFF_WIDTH),
                  full((SSD_WIDTH, D_MODEL)), full((DIFF_WIDTH, D_MODEL)), full((1, D_MODEL)),
                  full((D_MODEL, PEER_HEADS * PEER_QDIM)),
                  full((PEER_NKEYS, PEER_HALF)), full((PEER_NKEYS, PEER_HALF))],
        out_specs=[row(D_MODEL), row(D_MODEL),
                   pl.BlockSpec((PEER_HEADS, 2, PEER_NKEYS, tb), lambda i: (0, 0, 0, i))],
        out_shape=[jax.ShapeDtypeStruct((t, D_MODEL), F32),
                   jax.ShapeDtypeStruct((t, D_MODEL), F32),
                   jax.ShapeDtypeStruct((PEER_HEADS, 2, PEER_NKEYS, t), F32)],
        compiler_params=_cparams(("parallel",)),
        name="out_proj_peer_scores",
    )(x2, y_ssd, y_diff, wo[:SSD_WIDTH], wo[SSD_WIDTH:], norm_g.reshape(1, D_MODEL),
      wq.astype(BF16), k1.astype(BF16), k2.astype(BF16))


def _topk_rows(s, k):
    vals, idxs, _ = _select_rows([s[i:i + SUBLANES] for i in range(0, s.shape[0], SUBLANES)], k)
    return vals, idxs


def _tree(op, xs):
    while len(xs) > 1:
        xs = [op(xs[i], xs[i + 1]) for i in range(0, len(xs) - 1, 2)] + xs[len(xs) & ~1:]
    return xs[0]


def _all_sublanes(op, x):
    for shift in (4, 2, 1):
        x = op(x, pltpu.roll(x, shift, 0))
    return x


def _row_slabs(nslab):
    sub = lax.broadcasted_iota(jnp.int32, (SUBLANES, LANES), 0).astype(F32)
    return [sub + float(SUBLANES * i) for i in range(nslab)]


def _select_rows(slabs, k, payload=None):
    rows = _row_slabs(len(slabs))
    n = float(SUBLANES * len(slabs))
    vals, idxs, pays = [], [], []
    for _ in range(k):
        m = _all_sublanes(jnp.maximum, _tree(jnp.maximum, slabs))
        idx = _all_sublanes(jnp.minimum, _tree(
            jnp.minimum, [jnp.where(s == m, r, n) for s, r in zip(slabs, rows)]))
        hits = [r == idx for r in rows]
        vals.append(m)
        idxs.append(idx)
        if payload is not None:
            pays.append(_all_sublanes(jnp.maximum, _tree(
                jnp.maximum, [jnp.where(h, p, -1.0) for h, p in zip(hits, payload)])))
        slabs = [jnp.where(h, -jnp.inf, s) for h, s in zip(hits, slabs)]
    return vals, idxs, pays


def _stack_rows(reps):
    sub = lax.broadcasted_iota(jnp.int32, (SUBLANES, LANES), 0)
    out = reps[0]
    for j in range(1, len(reps)):
        out = jnp.where(sub == j, reps[j], out)
    return out


def _candidate_groups():
    groups = [(0, 0, SUBLANES, SUBLANES), (0, SUBLANES, SUBLANES, SUBLANES)]
    for a in range(1, SUBLANES):
        groups.append((a, 0, SUBLANES, min(SUBLANES, PEER_TOPK // (a + 1))))
    return groups


def _topk_tile(s1, s2):
    v1, i1 = _topk_rows(s1, PEER_TOPK)
    v2, i2 = _topk_rows(s2, PEER_TOPK)
    half = SUBLANES
    v2s = {0: _stack_rows(v2[:half]), half: _stack_rows(v2[half:])}
    i2s = {0: _stack_rows(i2[:half]), half: _stack_rows(i2[half:])}
    sub = lax.broadcasted_iota(jnp.int32, (SUBLANES, LANES), 0)
    cands, cidxs = [], []
    for a, b0, nrows, valid in _candidate_groups():
        c = v1[a] + v2s[b0]
        if valid < nrows:
            c = jnp.where(sub < valid, c, -jnp.inf)
        cands.append(c)
        cidxs.append(i1[a] * PEER_NKEYS + i2s[b0])
    cands.append(_stack_rows(v1[half:]) + v2[0])
    cidxs.append(_stack_rows(i1[half:]) * PEER_NKEYS + i2[0])
    tops, _, eids = _select_rows(cands, PEER_TOPK, payload=cidxs)
    es = [jnp.exp(t - tops[0]) for t in tops]
    total = _tree(jnp.add, es)
    gate = jnp.concatenate([_stack_rows(es[:half]), _stack_rows(es[half:])], axis=0) / total[0:1]
    eidx = jnp.concatenate([_stack_rows(eids[:half]), _stack_rows(eids[half:])], axis=0)
    return gate, eidx.astype(jnp.int32) * ROW_WORDS


def _topk_kernel(s_ref, eidx_ref, gate_ref, *, tb):
    def trip(j, carry):
        for u in range(TOPK_TILES_PER_TRIP):
            lanes = pl.ds(pl.multiple_of((j * TOPK_TILES_PER_TRIP + u) * LANES, LANES), LANES)
            gate, eidx = _topk_tile(s_ref[0, 0, :, lanes], s_ref[0, 1, :, lanes])
            gate_ref[0, :, lanes] = gate
            eidx_ref[0, :, lanes] = eidx
        return carry

    lax.fori_loop(0, tb // (LANES * TOPK_TILES_PER_TRIP), trip, 0)


def _topk(s_t, tb):
    t = s_t.shape[-1]
    out_spec = pl.BlockSpec((1, PEER_TOPK, tb), lambda h, i: (h, 0, i))
    return pl.pallas_call(
        functools.partial(_topk_kernel, tb=tb),
        grid=(PEER_HEADS, t // tb),
        in_specs=[pl.BlockSpec((1, 2, PEER_NKEYS, tb), lambda h, i: (h, 0, 0, i))],
        out_specs=[out_spec, out_spec],
        out_shape=[jax.ShapeDtypeStruct((PEER_HEADS, PEER_TOPK, t), jnp.int32),
                   jax.ShapeDtypeStruct((PEER_HEADS, PEER_TOPK, t), F32)],
        compiler_params=_cparams(("parallel", "parallel")),
        name="peer_topk",
    )(s_t)


def _table_layout(tab):
    tb = tab.astype(BF16).reshape(PEER_EXPERTS, ROW_WORDS, 2, LANES)
    words = lax.bitcast_convert_type(jnp.swapaxes(tb, -1, -2), jnp.uint32)
    return jnp.pad(words.reshape(PEER_EXPERTS * ROW_WORDS, LANES),
                   ((ROW_WORDS, ROW_WORDS), (0, 0)))


def _gather_rows(idx_ref, tab_ref, rows_ref, base):
    low = lax.broadcasted_iota(jnp.int32, (SUBLANES, LANES), 0) < ROW_WORDS
    tok_ref = idx_ref.at[pl.ds(base, PEER_PAIRS)]
    for k2 in range(PEER_PAIRS // 2):
        ra = tok_ref[2 * k2]
        rb = tok_ref[2 * k2 + 1]
        a = tab_ref[pl.ds(ra + ROW_WORDS, SUBLANES), :]
        b = tab_ref[pl.ds(rb, SUBLANES), :]
        rows_ref[k2 * SUBLANES:(k2 + 1) * SUBLANES, :] = jnp.where(low, a, b)


def _token_trips(token, rows_ref, tb):
    def trip(i, carry):
        for u in range(TOKENS_PER_TRIP):
            token(i * TOKENS_PER_TRIP + u, rows_ref.at[u % ROW_BUFFERS])
        return carry

    lax.fori_loop(0, tb // TOKENS_PER_TRIP, trip, 0)


def _chunk_diag():
    sub = lax.broadcasted_iota(jnp.int32, (ROW_CHUNKS, PEER_PAIRS * ROW_CHUNKS), 0)
    lane = lax.broadcasted_iota(jnp.int32, (ROW_CHUNKS, PEER_PAIRS * ROW_CHUNKS), 1)
    return (lane % ROW_CHUNKS) == sub


def _split_bf16(x):
    hi = x.astype(BF16)
    return hi, (x - hi.astype(F32)).astype(BF16)


def _peer_u_kernel(idx_ref, x_ref, gate_ref, seg_ref, tab_ref, w_ref, rows_ref, z_ref, *, tb):
    diag = _chunk_diag()

    def token(t, buf):
        _gather_rows(idx_ref, tab_ref, buf, t * PEER_PAIRS)
        rows = pltpu.bitcast(buf[...], BF16)
        z = lax.dot_general(x_ref[t].astype(BF16), rows, (((1,), (1,)), ((), ())),
                            preferred_element_type=F32)
        z_ref[pl.ds(t, 1), :] = jnp.sum(jnp.where(diag, z, 0.0), axis=0, keepdims=True)

    _token_trips(token, rows_ref, tb)
    hi, lo = _split_bf16(z_ref[...])
    act = (jnp.dot(hi, seg_ref[...], preferred_element_type=F32)
           + jnp.dot(lo, seg_ref[...], preferred_element_type=F32))
    gelu = 0.5 * act * (1.0 + lax.erf(act * (1.0 / math.sqrt(2.0))))
    w_ref[...] = gelu * gate_ref[...]


def _peer_v_kernel(idx_ref, w_ref, exp_ref, tab_ref, o_ref, rows_ref, whi_ref, wlo_ref, *, tb):
    hi, lo = _split_bf16(w_ref[...])
    whi_ref[...] = jnp.dot(hi, exp_ref[...], preferred_element_type=F32)
    wlo_ref[...] = jnp.dot(lo, exp_ref[...], preferred_element_type=F32)
    diag = _chunk_diag()

    def token(t, buf):
        _gather_rows(idx_ref, tab_ref, buf, t * PEER_PAIRS)
        rows = pltpu.bitcast(buf[...], BF16)
        wsel = jnp.concatenate([jnp.where(diag, whi_ref[pl.ds(t, 1), :], 0.0),
                                jnp.where(diag, wlo_ref[pl.ds(t, 1), :], 0.0)],
                               axis=0).astype(BF16)
        out = jnp.dot(wsel, rows, preferred_element_type=F32)
        o_ref[t] = out[:ROW_CHUNKS] + out[ROW_CHUNKS:]

    _token_trips(token, rows_ref, tb)


def _peer_specs(tb):
    idx = pl.BlockSpec((tb * PEER_PAIRS,), lambda i: (i,), memory_space=pltpu.SMEM)
    pairs = pl.BlockSpec((tb, PEER_PAIRS), lambda i: (i, 0))
    table = pl.BlockSpec((TABLE_ROWS, LANES), lambda i: (0, 0), pipeline_mode=pl.Buffered(1))
    rows = pl.BlockSpec((tb, ROW_CHUNKS, LANES), lambda i: (i, 0, 0))
    return idx, pairs, table, rows


def _peer_u(eidx4, xn3, gate, peer_u, tb):
    t = xn3.shape[0]
    seg = np.kron(np.eye(PEER_PAIRS, dtype=np.float32), np.ones((ROW_CHUNKS, 1), np.float32))
    idx, pairs, table, rows = _peer_specs(tb)
    return pl.pallas_call(
        functools.partial(_peer_u_kernel, tb=tb),
        grid=(t // tb,),
        in_specs=[idx, rows, pairs,
                  pl.BlockSpec((PEER_PAIRS * ROW_CHUNKS, PEER_PAIRS), lambda i: (0, 0)), table],
        out_specs=pairs,
        out_shape=jax.ShapeDtypeStruct((t, PEER_PAIRS), F32),
        scratch_shapes=[pltpu.VMEM((ROW_BUFFERS, PEER_PAIRS * ROW_WORDS, LANES), jnp.uint32),
                        pltpu.VMEM((tb, PEER_PAIRS * ROW_CHUNKS), F32)],
        compiler_params=_cparams(("arbitrary",), VMEM_LIMIT_TABLE),
        name="peer_u",
    )(eidx4, xn3, gate, jnp.asarray(seg, dtype=BF16), _table_layout(peer_u))


def _peer_v(eidx4, w, peer_v, tb):
    t = w.shape[0]
    expand = np.kron(np.eye(PEER_PAIRS, dtype=np.float32), np.ones((1, ROW_CHUNKS), np.float32))
    idx, pairs, table, rows = _peer_specs(tb)
    return pl.pallas_call(
        functools.partial(_peer_v_kernel, tb=tb),
        grid=(t // tb,),
        in_specs=[idx, pairs,
                  pl.BlockSpec((PEER_PAIRS, PEER_PAIRS * ROW_CHUNKS), lambda i: (0, 0)), table],
        out_specs=rows,
        out_shape=jax.ShapeDtypeStruct((t, ROW_CHUNKS, LANES), F32),
        scratch_shapes=[pltpu.VMEM((ROW_BUFFERS, PEER_PAIRS * ROW_WORDS, LANES), jnp.uint32),
                        pltpu.VMEM((tb, PEER_PAIRS * ROW_CHUNKS), F32),
                        pltpu.VMEM((tb, PEER_PAIRS * ROW_CHUNKS), F32)],
        compiler_params=_cparams(("arbitrary",), VMEM_LIMIT_TABLE),
        name="peer_v",
    )(eidx4, w, jnp.asarray(expand, dtype=BF16), _table_layout(peer_v))


def _final_kernel(h_ref, peer_ref, p_ref, wple_ref, gple_ref, ggate_ref, wgate_ref, o_ref):
    h = h_ref[...] + peer_ref[...]
    e = _rms(jnp.dot(p_ref[...].astype(BF16), wple_ref[...], preferred_element_type=F32),
             gple_ref[...])
    logits = jnp.dot(_rms(h, ggate_ref[...]).astype(BF16), wgate_ref[...],
                     preferred_element_type=F32)
    gate = 1.0 / (1.0 + jnp.exp(-logits))
    o_ref[...] = h + gate * e


def _final(h1, peer_out, p2, ple_proj, ple_norm_g, gate_norm_g, gate_w, tb):
    t = h1.shape[0]
    full = lambda shp: pl.BlockSpec(shp, lambda i: (0, 0))
    row = lambda w: pl.BlockSpec((tb, w), lambda i: (i, 0))
    return pl.pallas_call(
        _final_kernel,
        grid=(t // tb,),
        in_specs=[row(D_MODEL), row(D_MODEL), row(PLE_DIM), full((PLE_DIM, D_MODEL)),
                  full((1, D_MODEL)), full((1, D_MODEL)), full((D_MODEL, D_MODEL))],
        out_specs=row(D_MODEL),
        out_shape=jax.ShapeDtypeStruct((t, D_MODEL), F32),
        compiler_params=_cparams(("parallel",)),
        name="final_gate",
    )(h1, peer_out, p2, ple_proj.astype(BF16), ple_norm_g.reshape(1, D_MODEL),
      gate_norm_g.reshape(1, D_MODEL), gate_w.astype(BF16))


def _block_rows(t, want):
    tb = min(want, t)
    assert t % tb == 0
    return tb


def kernel(x, p, positions, norm_mix_g, w_in, conv_w, conv_b, dt_bias, a_log, d_skip, ssd_norm_g, q_norm_g, k_norm_g, lam_vecs, diff_norm_g, w_out, norm_ffn_g, peer_wq, peer_k1, peer_k2, peer_u, peer_v, ple_proj, ple_norm_g, ple_gate_norm_g, ple_gate_w):
    batch, seq, d = x.shape
    depth = w_in.shape[0]
    assert d == D_MODEL and seq % ATT_TILE == 0
    t = batch * seq
    h = x.reshape(t, D_MODEL)
    pos = positions.reshape(t, 1).astype(F32)
    for i in range(depth):
        lam_init = 0.8 - 0.6 * math.exp(-0.3 * i)
        z, xbc, q, k, v, dt = _in_proj(h, norm_mix_g[i], w_in[i], _block_rows(t, 512))
        y_ssd = _ssd(z, xbc, dt, conv_w[i], conv_b[i], dt_bias[i], a_log[i], d_skip[i],
                     ssd_norm_g[i], batch, seq)
        qr, kr = _qk_prep(q, k, pos, q_norm_g[i], k_norm_g[i], _block_rows(t, 512))
        y_diff = _attention(qr, kr, v, lam_vecs[i], diff_norm_g[i], batch, seq, lam_init)
        h1, xn, s_t = _out_proj(h, y_ssd, y_diff, w_out[i], norm_ffn_g[i], peer_wq[i],
                                peer_k1[i], peer_k2[i], _block_rows(t, 256))
        eidx_t, gate_t = _topk(s_t, _block_rows(t, 512))
        eidx4 = eidx_t.reshape(PEER_PAIRS, t).T.reshape(t * PEER_PAIRS)
        gate = gate_t.reshape(PEER_PAIRS, t).T
        tbp = _block_rows(t, 128)
        w = _peer_u(eidx4, xn.reshape(t, ROW_CHUNKS, LANES), gate, peer_u[i], tbp)
        peer_out = _peer_v(eidx4, w, peer_v[i], tbp).reshape(t, D_MODEL)
        h = _final(h1, peer_out, p[i].reshape(t, PLE_DIM), ple_proj[i], ple_norm_g[i],
                   ple_gate_norm_g[i], ple_gate_w[i], _block_rows(t, 256))
    return h.reshape(batch, seq, D_MODEL)
```
